```python
import jax, jax.numpy as jnp
from jax import lax
import numpy as np

D_MODEL = 1024
BATCH = 16
SEQ = 2048
DEPTH = 1

GRID_W = 64
CTX_LEN = 256
D_CONV = 1024
CONV_WIDTH = 31
N_HEADS = 8
QK_NOPE = 128
QK_ROPE = 64
V_DIM = 128
Q_LORA = 384
KV_LORA = 256
ROPE_THETA = 10000.0
Q_BLOCK = 128
ATTN_SCALE = (QK_NOPE + QK_ROPE) ** -0.5
D_FF = ((8 * D_MODEL // 3 + 255) // 256) * 256
EPS = 1e-6

_OFF_Q = 2 * D_CONV
_OFF_KV = _OFF_Q + Q_LORA
_OFF_KR = _OFF_KV + KV_LORA
_OFF_GATE = _OFF_KR + QK_ROPE
D_IN = _OFF_GATE + 2 * D_MODEL

kernel_name = "hybrid_conformer_mla_dit_block"


def rms_norm(x, g):
    xf = x.astype(jnp.float32)
    y = xf * lax.rsqrt(jnp.mean(jnp.square(xf), axis=-1, keepdims=True) + EPS)
    return (y * g.astype(jnp.float32)).astype(x.dtype)


def layer_norm(x, g, b):
    xf = x.astype(jnp.float32)
    mu = jnp.mean(xf, axis=-1, keepdims=True)
    var = jnp.mean(jnp.square(xf - mu), axis=-1, keepdims=True)
    y = (xf - mu) * lax.rsqrt(var + EPS)
    return (y * g.astype(jnp.float32) + b.astype(jnp.float32)).astype(x.dtype)


def modulate(h, shift, scale):
    return h * (1 + scale) + shift


def axial_angles(rows):
    row = jnp.repeat(jnp.arange(rows, dtype=jnp.float32), GRID_W)
    col = jnp.tile(jnp.arange(GRID_W, dtype=jnp.float32), rows)
    axis_dim = QK_ROPE // 2
    inv_freq = ROPE_THETA ** (-jnp.arange(0, axis_dim, 2, dtype=jnp.float32) / axis_dim)
    return row[:, None] * inv_freq, col[:, None] * inv_freq


def rotate_segment(x, ang):
    x1, x2 = jnp.split(x, 2, axis=-1)
    cos = jnp.cos(ang).astype(x.dtype)
    sin = jnp.sin(ang).astype(x.dtype)
    return jnp.concatenate([x1 * cos - x2 * sin, x2 * cos + x1 * sin], axis=-1)


def axial_rope(x, ang_row, ang_col):
    xr, xc = jnp.split(x, 2, axis=-1)
    return jnp.concatenate([rotate_segment(xr, ang_row), rotate_segment(xc, ang_col)], axis=-1)


def mla_query(q_a, g_q, w_uq):
    bsz, n, _ = q_a.shape
    q = (rms_norm(q_a, g_q) @ w_uq).reshape(bsz, n, N_HEADS, QK_NOPE + QK_ROPE)
    return q[..., :QK_NOPE], q[..., QK_NOPE:]


def mla_kv(kv_a, g_kv, w_ukv):
    bsz, n, _ = kv_a.shape
    kv = (rms_norm(kv_a, g_kv) @ w_ukv).reshape(bsz, n, N_HEADS, QK_NOPE + V_DIM)
    return kv[..., :QK_NOPE], kv[..., QK_NOPE:]


def mla_attend(q_nope, q_rope, k_nope, k_rope, v):
    s = (jnp.einsum('bqhd,bkhd->bhqk', q_nope, k_nope)
         + jnp.einsum('bqhr,bkr->bhqk', q_rope, k_rope))
    p = jax.nn.softmax(s.astype(jnp.float32) * ATTN_SCALE, axis=-1).astype(v.dtype)
    return jnp.einsum('bhqk,bkhd->bqhd', p, v)


def blocked_mla(q_nope, q_rope, k_nope, k_rope, v):
    bsz, n = q_nope.shape[:2]
    nb = n // Q_BLOCK

    def to_blocks(t):
        return jnp.moveaxis(t.reshape(bsz, nb, Q_BLOCK, *t.shape[2:]), 1, 0)

    out = lax.map(lambda qs: mla_attend(qs[0], qs[1], k_nope, k_rope, v),
                  (to_blocks(q_nope), to_blocks(q_rope)))
    return jnp.moveaxis(out, 0, 1).reshape(bsz, n, N_HEADS * V_DIM)


def conformer_conv(conv_in, w_dw, b_dw, ln_g, ln_b, w_pw):
    a, g = jnp.split(conv_in, 2, axis=-1)
    u = a * jax.nn.sigmoid(g)
    u = lax.conv_general_dilated(
        u, w_dw[:, None, :], window_strides=(1,),
        padding=((CONV_WIDTH // 2, CONV_WIDTH // 2),),
        dimension_numbers=('NWC', 'WIO', 'NWC'),
        feature_group_count=D_CONV) + b_dw
    u = jax.nn.silu(layer_norm(u, ln_g, ln_b))
    return u @ w_pw


def merge_branches(y_conv, y_mla, gates, w_out):
    g_conv, g_mla = jnp.split(gates, 2, axis=-1)
    return (jax.nn.sigmoid(g_conv) * y_conv + jax.nn.sigmoid(g_mla) * y_mla) @ w_out


def swiglu(h, w_13, w_2):
    a, b = jnp.split(h @ w_13, 2, axis=-1)
    return (jax.nn.silu(a) * b) @ w_2


def setup_inputs(seed: int = 0) -> dict:
    key = jax.random.key(seed)
    ks = jax.random.split(key, 24)
    f32 = jnp.float32

    def nrm(k, shape, scale=1.0):
        return scale * jax.random.normal(k, shape, f32)

    def gain(k, n):
        return 1.0 + 0.01 * jax.random.normal(k, (DEPTH, n), f32)

    return {
        "x": nrm(ks[0], (BATCH, SEQ, D_MODEL)),
        "c": nrm(ks[1], (BATCH, D_MODEL)),
        "ctx": nrm(ks[2], (BATCH, CTX_LEN, D_MODEL)),
        "c_ctx": nrm(ks[3], (D_MODEL,)),
        "w_mod": nrm(ks[4], (DEPTH, D_MODEL, 6 * D_MODEL), 0.5 * D_MODEL ** -0.5),
        "b_mod": nrm(ks[5], (DEPTH, 6 * D_MODEL), 0.01),
        "g_mix": gain(ks[6], D_MODEL),
        "g_ffn": gain(ks[7], D_MODEL),
        "w_in": nrm(ks[8], (DEPTH, D_MODEL, D_IN), D_MODEL ** -0.5),
        "g_q": gain(ks[9], Q_LORA),
        "g_kv": gain(ks[10], KV_LORA),
        "w_uq": nrm(ks[11], (DEPTH, Q_LORA, N_HEADS * (QK_NOPE + QK_ROPE)), Q_LORA ** -0.5),
        "w_ukv": nrm(ks[12], (DEPTH, KV_LORA, N_HEADS * (QK_NOPE + V_DIM)), KV_LORA ** -0.5),
        "w_o_mla": nrm(ks[13], (DEPTH, N_HEADS * V_DIM, D_MODEL), (N_HEADS * V_DIM) ** -0.5),
        "w_dw": nrm(ks[14], (DEPTH, CONV_WIDTH, D_CONV), CONV_WIDTH ** -0.5),
        "b_dw": nrm(ks[15], (DEPTH, D_CONV), 0.01),
        "ln_g": gain(ks[16], D_CONV),
        "ln_b": nrm(ks[17], (DEPTH, D_CONV), 0.01),
        "w_pw": nrm(ks[18], (DEPTH, D_CONV, D_MODEL), D_CONV ** -0.5),
        "w_out": nrm(ks[19], (DEPTH, D_MODEL, D_MODEL), D_MODEL ** -0.5),
        "w_13": nrm(ks[20], (DEPTH, D_MODEL, 2 * D_FF), D_MODEL ** -0.5),
        "w_2": nrm(ks[21], (DEPTH, D_FF, D_MODEL), D_FF ** -0.5),
        "g_final": 1.0 + 0.01 * jax.random.normal(ks[22], (D_MODEL,), f32),
    }


def reference(x, c, ctx, c_ctx, w_mod, b_mod, g_mix, g_ffn, w_in, g_q, g_kv, w_uq, w_ukv,
              w_o_mla, w_dw, b_dw, ln_g, ln_b, w_pw, w_out, w_13, w_2, g_final):
    bsz, n, _ = x.shape
    ROWS = n // GRID_W
    ang_r, ang_c = axial_angles(ROWS)
    silu_c = jax.nn.silu(c)
    silu_cc = jax.nn.silu(c_ctx)

    for l in range(DEPTH):
        last = l == DEPTH - 1
        mx = jnp.split((silu_c @ w_mod[l] + b_mod[l])[:, None, :], 6, axis=-1)
        mc = jnp.split(silu_cc @ w_mod[l] + b_mod[l], 6, axis=-1)

        hx = modulate(rms_norm(x, g_mix[l]), mx[0], mx[1])
        hc = modulate(rms_norm(ctx, g_mix[l]), mc[0], mc[1])

        zx = hx @ w_in[l]
        conv_x, qa_x, kva_x, kr_x, gt_x = jnp.split(
            zx, [_OFF_Q, _OFF_KV, _OFF_KR, _OFF_GATE], axis=-1)
        qn_x, qr_x = mla_query(qa_x, g_q[l], w_uq[l])
        kn_x, v_x = mla_kv(kva_x, g_kv[l], w_ukv[l])
        qr_x = axial_rope(qr_x, ang_r[:, None], ang_c[:, None])
        kr_x = axial_rope(kr_x, ang_r, ang_c)

        if last:
            zc = hc @ w_in[l][:, _OFF_KV:_OFF_GATE]
            kva_c, kr_c = jnp.split(zc, [KV_LORA], axis=-1)
        else:
            zc = hc @ w_in[l]
            conv_c, qa_c, kva_c, kr_c, gt_c = jnp.split(
                zc, [_OFF_Q, _OFF_KV, _OFF_KR, _OFF_GATE], axis=-1)
        kn_c, v_c = mla_kv(kva_c, g_kv[l], w_ukv[l])

        k_nope = jnp.concatenate([kn_c, kn_x], axis=1)
        k_rope = jnp.concatenate([kr_c, kr_x], axis=1)
        v_all = jnp.concatenate([v_c, v_x], axis=1)
        att_x = blocked_mla(qn_x, qr_x, k_nope, k_rope, v_all) @ w_o_mla[l]
        y_conv_x = conformer_conv(conv_x, w_dw[l], b_dw[l], ln_g[l], ln_b[l], w_pw[l])
        x = x + mx[2] * merge_branches(y_conv_x, att_x, gt_x, w_out[l])

        hx2 = modulate(rms_norm(x, g_ffn[l]), mx[3], mx[4])
        x = x + mx[5] * swiglu(hx2, w_13[l], w_2[l])

        if not last:
            qn_c, qr_c = mla_query(qa_c, g_q[l], w_uq[l])
            att_c = mla_attend(qn_c, qr_c, kn_c, kr_c, v_c).reshape(
                bsz, ctx.shape[1], N_HEADS * V_DIM) @ w_o_mla[l]
            y_conv_c = conformer_conv(conv_c, w_dw[l], b_dw[l], ln_g[l], ln_b[l], w_pw[l])
            ctx = ctx + mc[2] * merge_branches(y_conv_c, att_c, gt_c, w_out[l])
            hc2 = modulate(rms_norm(ctx, g_ffn[l]), mc[3], mc[4])
            ctx = ctx + mc[5] * swiglu(hc2, w_13[l], w_2[l])

    return rms_norm(x, g_final)
```

```python
import functools

import jax
import jax.numpy as jnp
import numpy as np
from jax import lax
from jax.experimental import pallas as pl
from jax.experimental.pallas import tpu as pltpu

F32 = jnp.float32
BF16 = jnp.bfloat16

D_MODEL = 1024
N_HEADS = 8
QK_NOPE = 128
QK_ROPE = 64
V_DIM = 128
Q_LORA = 384
KV_LORA = 256
D_CONV = 1024
CONV_WIDTH = 31
CONV_HALO = 16
GRID_W = 64
ROPE_THETA = 10000.0
EPS = 1e-6
ATTN_SCALE = (QK_NOPE + QK_ROPE) ** -0.5
HEAD_PAD = 256
LANES = 128
MOD_ROWS = 24

VMEM_LIMIT = 56 * 1024 * 1024

_A0, _G0, _GT0, _QA0, _KVA0, _KR0, _KRS0, _WA_COLS = 0, 1024, 2048, 4096, 4480, 4736, 4864, 4992


def _resident(shape):
    nd = len(shape)
    return pl.BlockSpec(shape, lambda *_: (0,) * nd, pipeline_mode=pl.Buffered(1))


def _params(sem):
    return pltpu.CompilerParams(dimension_semantics=sem, vmem_limit_bytes=VMEM_LIMIT)


def _rms(x, g):
    return x * lax.rsqrt(jnp.mean(x * x, axis=-1, keepdims=True) + EPS) * g


def _silu(x):
    return x * jax.nn.sigmoid(x)


def _dot(a, b):
    return jnp.dot(a, b, preferred_element_type=F32)


def _mod_kernel(cc_ref, w_ref, b_ref, o_ref):
    s = _silu(cc_ref[...])
    o_ref[...] = jnp.dot(s, w_ref[...], preferred_element_type=F32,
                         precision=lax.Precision.HIGHEST) + b_ref[...]


def _mod_call(cc, w_mod, b_mod):
    n = w_mod.shape[1]
    tn = 1024
    return pl.pallas_call(
        _mod_kernel,
        grid=(n // tn,),
        in_specs=[pl.BlockSpec((MOD_ROWS, D_MODEL), lambda j: (0, 0)),
                  pl.BlockSpec((D_MODEL, tn), lambda j: (0, j)),
                  pl.BlockSpec((1, tn), lambda j: (0, j))],
        out_specs=pl.BlockSpec((MOD_ROWS, tn), lambda j: (0, j)),
        out_shape=jax.ShapeDtypeStruct((MOD_ROWS, n), F32),
        compiler_params=_params(("arbitrary",)),
        name="mod",
    )(cc, w_mod, b_mod)


def _store_k(k_ref, kn, kr):
    krb = kr.astype(BF16)
    for h in range(N_HEADS):
        k_ref[0, :, h * HEAD_PAD:h * HEAD_PAD + QK_NOPE] = (
            kn[:, h * QK_NOPE:(h + 1) * QK_NOPE].astype(BF16))
        k_ref[0, :, h * HEAD_PAD + QK_NOPE:(h + 1) * HEAD_PAD] = krb


def _inproj_kernel(x_ref, mod_ref, tab_ref, gmix_ref, gq_ref, gkv_ref,
                   wa_ref, wq1_ref, wq2_ref, wkv_ref,
                   u_ref, gate_ref, q_ref, k_ref, v_ref):
    m = mod_ref[0]
    shift, scale = m[:, 0:D_MODEL], m[:, D_MODEL:2 * D_MODEL]
    hx = (_rms(x_ref[0], gmix_ref[...]) * (1.0 + scale) + shift).astype(BF16)

    cn = 512
    for j in range(0, D_CONV, cn):
        a = _dot(hx, wa_ref[:, _A0 + j:_A0 + j + cn])
        g = _dot(hx, wa_ref[:, _G0 + j:_G0 + j + cn])
        u_ref[0, :, j:j + cn] = a * jax.nn.sigmoid(g)
    for j in range(0, 2 * D_MODEL, cn):
        gate_ref[0, :, j:j + cn] = jax.nn.sigmoid(_dot(hx, wa_ref[:, _GT0 + j:_GT0 + j + cn]))

    z = _dot(hx, wa_ref[:, _QA0:_WA_COLS])
    qa = z[:, 0:Q_LORA]
    kva = z[:, Q_LORA:Q_LORA + KV_LORA]
    kr_a = z[:, _KR0 - _QA0:_KR0 - _QA0 + LANES]
    kr_b = z[:, _KRS0 - _QA0:_KRS0 - _QA0 + LANES]

    tab = tab_ref[...]
    tq_c, tq_s = tab[:, 0:LANES], tab[:, LANES:2 * LANES]
    tk_c, tk_s = tab[:, 2 * LANES:3 * LANES], tab[:, 3 * LANES:4 * LANES]

    qn = _rms(qa, gq_ref[...]).astype(BF16)
    qf = _dot(qn, wq1_ref[...])
    qs = _dot(qn, wq2_ref[...])
    for h in range(N_HEADS):
        c0 = h * HEAD_PAD
        q_ref[0, :, c0:c0 + QK_NOPE] = (qf[:, c0:c0 + QK_NOPE] * ATTN_SCALE).astype(BF16)
        q_ref[0, :, c0 + QK_NOPE:c0 + HEAD_PAD] = (
            qf[:, c0 + QK_NOPE:c0 + HEAD_PAD] * tq_c
            + qs[:, h * LANES:(h + 1) * LANES] * tq_s).astype(BF16)

    kvn = _rms(kva, gkv_ref[...]).astype(BF16)
    kv = _dot(kvn, wkv_ref[...])
    _store_k(k_ref, kv[:, 0:N_HEADS * QK_NOPE], kr_a * tk_c + kr_b * tk_s)
    v_ref[0] = kv[:, N_HEADS * QK_NOPE:].astype(BF16)


def _inproj_call(x, mod3, tab, g_mix, g_q, g_kv, wa, wq1, wq2, wkv, tm):
    B, S, _ = x.shape
    row = lambda w: pl.BlockSpec((1, tm, w), lambda b, i: (b, i, 0))
    return pl.pallas_call(
        _inproj_kernel,
        grid=(B, S // tm),
        in_specs=[row(D_MODEL),
                  pl.BlockSpec((1, 1, 6 * D_MODEL), lambda b, i: (b, 0, 0)),
                  pl.BlockSpec((tm, 4 * LANES), lambda b, i: (i, 0)),
                  _resident((1, D_MODEL)), _resident((1, Q_LORA)), _resident((1, KV_LORA)),
                  _resident(wa.shape), _resident(wq1.shape), _resident(wq2.shape),
                  _resident(wkv.shape)],
        out_specs=[row(D_CONV), row(2 * D_MODEL), row(N_HEADS * HEAD_PAD),
                   row(N_HEADS * HEAD_PAD), row(N_HEADS * V_DIM)],
        out_shape=[jax.ShapeDtypeStruct((B, S, D_CONV), F32),
                   jax.ShapeDtypeStruct((B, S, 2 * D_MODEL), F32),
                   jax.ShapeDtypeStruct((B, S, N_HEADS * HEAD_PAD), BF16),
                   jax.ShapeDtypeStruct((B, S, N_HEADS * HEAD_PAD), BF16),
                   jax.ShapeDtypeStruct((B, S, N_HEADS * V_DIM), BF16)],
        compiler_params=_params(("arbitrary", "arbitrary")),
        name="inproj",
    )(x, mod3, tab, g_mix, g_q, g_kv, wa, wq1, wq2, wkv)


def _ctxproj_kernel(c_ref, mod_ref, gmix_ref, gkv_ref, wc_ref, wkv_ref, k_ref, v_ref):
    m = mod_ref[0]
    shift, scale = m[:, 0:D_MODEL], m[:, D_MODEL:2 * D_MODEL]
    hc = (_rms(c_ref[0], gmix_ref[...]) * (1.0 + scale) + shift).astype(BF16)
    z = _dot(hc, wc_ref[...])
    kvn = _rms(z[:, 0:KV_LORA], gkv_ref[...]).astype(BF16)
    kv = _dot(kvn, wkv_ref[...])
    _store_k(k_ref, kv[:, 0:N_HEADS * QK_NOPE], z[:, KV_LORA:KV_LORA + LANES])
    v_ref[0] = kv[:, N_HEADS * QK_NOPE:].astype(BF16)


def _ctxproj_call(ctx, mod3, g_mix, g_kv, wc, wkv):
    B, C, _ = ctx.shape
    row = lambda w: pl.BlockSpec((1, C, w), lambda b: (b, 0, 0))
    return pl.pallas_call(
        _ctxproj_kernel,
        grid=(B,),
        in_specs=[row(D_MODEL),
                  pl.BlockSpec((1, 1, 6 * D_MODEL), lambda b: (MOD_ROWS - 8, 0, 0)),
                  _resident((1, D_MODEL)), _resident((1, KV_LORA)),
                  _resident(wc.shape), _resident(wkv.shape)],
        out_specs=[row(N_HEADS * HEAD_PAD), row(N_HEADS * V_DIM)],
        out_shape=[jax.ShapeDtypeStruct((B, C, N_HEADS * HEAD_PAD), BF16),
                   jax.ShapeDtypeStruct((B, C, N_HEADS * V_DIM), BF16)],
        compiler_params=_params(("arbitrary",)),
        name="ctxproj",
    )(ctx, mod3, g_mix, g_kv, wc, wkv)


def _attn_kernel(q_ref, kx_ref, kc_ref, vx_ref, vc_ref, o_ref):
    nt = (((1,), (1,)), ((), ()))
    q = q_ref[0]
    sx = lax.dot_general(q, kx_ref[0], nt, preferred_element_type=F32)
    sc = lax.dot_general(q, kc_ref[0], nt, preferred_element_type=F32)
    m = jnp.maximum(jnp.max(sx, axis=-1, keepdims=True), jnp.max(sc, axis=-1, keepdims=True))
    px = jnp.exp(sx - m)
    pc = jnp.exp(sc - m)
    l = jnp.sum(px, axis=-1, keepdims=True) + jnp.sum(pc, axis=-1, keepdims=True)
    o = _dot(px.astype(BF16), vx_ref[0]) + _dot(pc.astype(BF16), vc_ref[0])
    o_ref[0] = (o / l).astype(o_ref.dtype)


def _attn_call(q, kx, kc, vx, vc, tq):
    B, S, _ = q.shape
    C = kc.shape[1]
    return pl.pallas_call(
        _attn_kernel,
        grid=(B, N_HEADS, S // tq),
        in_specs=[pl.BlockSpec((1, tq, HEAD_PAD), lambda b, h, i: (b, i, h)),
                  pl.BlockSpec((1, S, HEAD_PAD), lambda b, h, i: (b, 0, h)),
                  pl.BlockSpec((1, C, HEAD_PAD), lambda b, h, i: (b, 0, h)),
                  pl.BlockSpec((1, S, V_DIM), lambda b, h, i: (b, 0, h)),
                  pl.BlockSpec((1, C, V_DIM), lambda b, h, i: (b, 0, h))],
        out_specs=pl.BlockSpec((1, tq, V_DIM), lambda b, h, i: (b, i, h)),
        out_shape=jax.ShapeDtypeStruct((B, S, N_HEADS * V_DIM), BF16),
        compiler_params=_params(("arbitrary", "arbitrary", "arbitrary")),
        name="attn",
    )(q, kx, kc, vx, vc)


def _conv_kernel(u_ref, w_ref, b_ref, lg_ref, lb_ref, o_ref, pad_ref, y_ref, *, tc, rows):
    S = u_ref.shape[1]
    i = pl.program_id(1)
    sub = D_CONV // LANES

    @pl.when(i == 0)
    def _():
        zeros = jnp.zeros((CONV_HALO * sub, LANES), F32)
        pad_ref[0:CONV_HALO * sub, :] = zeros
        pad_ref[(CONV_HALO + S) * sub:(2 * CONV_HALO + S) * sub, :] = zeros
        for s in range(sub):
            pad_ref[pl.ds(CONV_HALO * sub + s, S, stride=sub), :] = u_ref[0, :, s * LANES:(s + 1) * LANES]

    bias, lg, lb = b_ref[...], lg_ref[...], lb_ref[...]
    off = CONV_HALO - CONV_WIDTH // 2

    def chunk(r, carry):
        r0 = i * tc + r * rows
        acc = jnp.zeros((rows, sub, LANES), F32)
        for k in range(CONV_WIDTH):
            win = pad_ref[pl.ds(pl.multiple_of((r0 + (off + k)) * sub, sub), rows * sub), :]
            acc = acc + win.reshape(rows, sub, LANES) * w_ref[k]
        y_ref[...] = (acc + bias).reshape(rows * sub, LANES)
        y = jnp.concatenate([y_ref[pl.ds(s, rows, stride=sub), :] for s in range(sub)], axis=-1)
        mu = jnp.mean(y, axis=-1, keepdims=True)
        yc = y - mu
        var = jnp.mean(yc * yc, axis=-1, keepdims=True)
        y = yc * lax.rsqrt(var + EPS) * lg + lb
        o_ref[0, pl.ds(pl.multiple_of(r * rows, rows), rows), :] = _silu(y).astype(o_ref.dtype)
        return carry

    lax.fori_loop(0, tc // rows, chunk, 0)


def _conv_call(u, w3, b3, ln_g, ln_b, tc, rows):
    B, S, _ = u.shape
    sub = D_CONV // LANES
    return pl.pallas_call(
        functools.partial(_conv_kernel, tc=tc, rows=rows),
        grid=(B, S // tc),
        in_specs=[pl.BlockSpec((1, S, D_CONV), lambda b, i: (b, 0, 0)),
                  _resident(w3.shape), _resident(b3.shape), _resident((1, D_CONV)),
                  _resident((1, D_CONV))],
        out_specs=pl.BlockSpec((1, tc, D_CONV), lambda b, i: (b, i, 0)),
        out_shape=jax.ShapeDtypeStruct((B, S, D_CONV), BF16),
        scratch_shapes=[pltpu.VMEM(((S + 2 * CONV_HALO) * sub, LANES), F32),
                        pltpu.VMEM((rows * sub, LANES), F32)],
        compiler_params=_params(("arbitrary", "arbitrary")),
        name="conv",
    )(u, w3, b3, ln_g, ln_b)


def _merge_kernel(cv_ref, att_ref, gate_ref, x_ref, mod_ref, wpw_ref, wo_ref, wout_ref, o_ref):
    m = mod_ref[0]
    gate_mix = m[:, 2 * D_MODEL:3 * D_MODEL]
    y_conv = _dot(cv_ref[0], wpw_ref[...])
    y_mla = _dot(att_ref[0], wo_ref[...])
    g = gate_ref[0]
    merged = (g[:, 0:D_MODEL] * y_conv + g[:, D_MODEL:] * y_mla).astype(BF16)
    o_ref[0] = x_ref[0] + gate_mix * _dot(merged, wout_ref[...])


def _merge_call(cv, att, gates, x, mod3, w_pw, w_o, w_out, tm):
    B, S, _ = x.shape
    row = lambda w: pl.BlockSpec((1, tm, w), lambda b, i: (b, i, 0))
    return pl.pallas_call(
        _merge_kernel,
        grid=(B, S // tm),
        in_specs=[row(D_CONV), row(N_HEADS * V_DIM), row(2 * D_MODEL), row(D_MODEL),
                  pl.BlockSpec((1, 1, 6 * D_MODEL), lambda b, i: (b, 0, 0)),
                  _resident(w_pw.shape), _resident(w_o.shape), _resident(w_out.shape)],
        out_specs=row(D_MODEL),
        out_shape=jax.ShapeDtypeStruct((B, S, D_MODEL), F32),
        compiler_params=_params(("arbitrary", "arbitrary")),
        name="merge",
    )(cv, att, gates, x, mod3, w_pw, w_o, w_out)


def _ffn_kernel(x_ref, mod_ref, gffn_ref, gfin_ref, w13_ref, w2_ref, o_ref, h_ref, *, d_ff, cn):
    m = mod_ref[0]
    shift, scale = m[:, 3 * D_MODEL:4 * D_MODEL], m[:, 4 * D_MODEL:5 * D_MODEL]
    gate_ffn = m[:, 5 * D_MODEL:6 * D_MODEL]
    x = x_ref[0]
    h2 = (_rms(x, gffn_ref[...]) * (1.0 + scale) + shift).astype(BF16)
    for j in range(0, d_ff, cn):
        a = _dot(h2, w13_ref[:, j:j + cn])
        b = _dot(h2, w13_ref[:, d_ff + j:d_ff + j + cn])
        h_ref[:, j:j + cn] = (_silu(a) * b).astype(BF16)
    y = x + gate_ffn * _dot(h_ref[...], w2_ref[...])
    o_ref[0] = _rms(y, gfin_ref[...])


def _ffn_call(x1, mod3, g_ffn, g_final, w13, w2, tm, cn):
    B, S, _ = x1.shape
    d_ff = w2.shape[0]
    row = lambda w: pl.BlockSpec((1, tm, w), lambda b, i: (b, i, 0))
    return pl.pallas_call(
        functools.partial(_ffn_kernel, d_ff=d_ff, cn=cn),
        grid=(B, S // tm),
        in_specs=[row(D_MODEL),
                  pl.BlockSpec((1, 1, 6 * D_MODEL), lambda b, i: (b, 0, 0)),
                  _resident((1, D_MODEL)), _resident((1, D_MODEL)),
                  _resident(w13.shape), _resident(w2.shape)],
        out_specs=row(D_MODEL),
        out_shape=jax.ShapeDtypeStruct((B, S, D_MODEL), F32),
        scratch_shapes=[pltpu.VMEM((tm, d_ff), BF16)],
        compiler_params=_params(("arbitrary", "arbitrary")),
        name="ffn",
    )(x1, mod3, g_ffn, g_final, w13, w2)


def _rope_tables(S):
    rows = S // GRID_W
    row = jnp.repeat(jnp.arange(rows, dtype=F32), GRID_W)
    col = jnp.tile(jnp.arange(GRID_W, dtype=F32), rows)
    axis_dim = QK_ROPE // 2
    inv_freq = ROPE_THETA ** (-jnp.arange(0, axis_dim, 2, dtype=F32) / axis_dim)
    ar, ac = row[:, None] * inv_freq, col[:, None] * inv_freq
    cos = jnp.concatenate([jnp.cos(ar), jnp.cos(ar), jnp.cos(ac), jnp.cos(ac)], axis=-1)
    sin = jnp.concatenate([-jnp.sin(ar), jnp.sin(ar), -jnp.sin(ac), jnp.sin(ac)], axis=-1)
    z = jnp.zeros((S, LANES - QK_ROPE), F32)
    return jnp.concatenate([ATTN_SCALE * cos, z, ATTN_SCALE * sin, z, cos, z, sin, z], axis=-1)


_ROPE_SWAP = np.concatenate([np.arange(16, 32), np.arange(0, 16),
                             np.arange(48, 64), np.arange(32, 48)])


def _prep_weights(w_in, w_uq, w_ukv):
    off_q, off_kv = 2 * D_CONV, 2 * D_CONV + Q_LORA
    off_kr, off_gate = off_kv + KV_LORA, off_kv + KV_LORA + QK_ROPE
    w_kr = w_in[:, off_kr:off_gate]
    z64 = jnp.zeros((D_MODEL, LANES - QK_ROPE), w_in.dtype)
    wa = jnp.concatenate([w_in[:, 0:off_q], w_in[:, off_gate:], w_in[:, off_q:off_kr],
                          w_kr, z64, w_kr[:, _ROPE_SWAP], z64], axis=1).astype(BF16)
    wc = jnp.concatenate([w_in[:, off_kv:off_gate], z64], axis=1).astype(BF16)

    uq = w_uq.reshape(Q_LORA, N_HEADS, QK_NOPE + QK_ROPE)
    nope, rope = uq[:, :, :QK_NOPE], uq[:, :, QK_NOPE:]
    zq = jnp.zeros((Q_LORA, N_HEADS, LANES - QK_ROPE), w_uq.dtype)
    wq1 = jnp.concatenate([nope, rope, zq], axis=2).reshape(Q_LORA, N_HEADS * HEAD_PAD).astype(BF16)
    wq2 = jnp.concatenate([rope[:, :, _ROPE_SWAP], zq], axis=2).reshape(Q_LORA, N_HEADS * LANES).astype(BF16)

    ukv = w_ukv.reshape(KV_LORA, N_HEADS, QK_NOPE + V_DIM)
    wkv = jnp.concatenate([ukv[:, :, :QK_NOPE].reshape(KV_LORA, N_HEADS * QK_NOPE),
                           ukv[:, :, QK_NOPE:].reshape(KV_LORA, N_HEADS * V_DIM)], axis=1).astype(BF16)
    return wa, wc, wq1, wq2, wkv


def kernel(x, c, ctx, c_ctx, w_mod, b_mod, g_mix, g_ffn, w_in, g_q, g_kv, w_uq, w_ukv,
           w_o_mla, w_dw, b_dw, ln_g, ln_b, w_pw, w_out, w_13, w_2, g_final):
    B, S, _ = x.shape
    assert w_mod.shape[0] == 1, "single-layer block"
    assert B + 1 <= MOD_ROWS - 7

    wa, wc, wq1, wq2, wkv = _prep_weights(w_in[0], w_uq[0], w_ukv[0])
    tab = _rope_tables(S)
    w3 = w_dw[0].reshape(CONV_WIDTH, D_CONV // LANES, LANES)
    b3 = b_dw.reshape(D_CONV // LANES, LANES)

    cc = jnp.zeros((MOD_ROWS, D_MODEL), F32).at[0:B].set(c).at[MOD_ROWS - 8].set(c_ctx)
    mod3 = _mod_call(cc, w_mod[0], b_mod).reshape(MOD_ROWS, 1, 6 * D_MODEL)

    u, gates, q, kx, vx = _inproj_call(x, mod3, tab, g_mix, g_q, g_kv, wa, wq1, wq2, wkv, tm=256)
    kc, vc = _ctxproj_call(ctx, mod3, g_mix, g_kv, wc, wkv)
    att = _attn_call(q, kx, kc, vx, vc, tq=512)
    cv = _conv_call(u, w3, b3, ln_g, ln_b, tc=512, rows=32)
    x1 = _merge_call(cv, att, gates, x, mod3, w_pw[0].astype(BF16), w_o_mla[0].astype(BF16),
                     w_out[0].astype(BF16), tm=512)
    return _ffn_call(x1, mod3, g_ffn, g_final.reshape(1, D_MODEL), w_13[0].astype(BF16),
                     w_2[0].astype(BF16), tm=512, cn=256)
```

```python
import functools

import jax
import jax.numpy as jnp
import numpy as np
from jax import lax
from jax.experimental import pallas as pl
from jax.experimental.pallas import tpu as pltpu

F32 = jnp.float32
BF16 = jnp.bfloat16

D_MODEL = 1024
N_HEADS = 8
QK_NOPE = 128
QK_ROPE = 64
V_DIM = 128
Q_LORA = 384
KV_LORA = 256
D_CONV = 1024
CONV_WIDTH = 31
CONV_HALO = 16
GRID_W = 64
ROPE_THETA = 10000.0
EPS = 1e-6
ATTN_SCALE = (QK_NOPE + QK_ROPE) ** -0.5
Q_SCALE = ATTN_SCALE * float(np.log2(np.e))
HEAD_PAD = 256
Q_BLK = 256
K_BLK = 256
ATTN_LAG = 6
LANES = 128
MOD_ROWS = 24

VMEM_LIMIT = 56 * 1024 * 1024

_A0, _G0, _GT0, _QA0, _KVA0, _KR0, _KRS0, _WA_COLS = 0, 1024, 2048, 4096, 4480, 4736, 4864, 4992

_NT = (((1,), (1,)), ((), ()))


def _resident(shape):
    nd = len(shape)
    return pl.BlockSpec(shape, lambda *_: (0,) * nd, pipeline_mode=pl.Buffered(1))


def _params(sem):
    return pltpu.CompilerParams(dimension_semantics=sem, vmem_limit_bytes=VMEM_LIMIT)


def _rms(x, g):
    return x * lax.rsqrt(jnp.mean(x * x, axis=-1, keepdims=True) + EPS) * g


def _silu(x):
    return x * jax.nn.sigmoid(x)


def _dot(a, b):
    return jnp.dot(a, b, preferred_element_type=F32)


def _dot_nt(a, b):
    return lax.dot_general(a, b, _NT, preferred_element_type=F32)


def _mod_kernel(cc_ref, w_ref, b_ref, o_ref):
    s = _silu(cc_ref[...])
    o_ref[...] = jnp.dot(s, w_ref[...], preferred_element_type=F32,
                         precision=lax.Precision.HIGHEST) + b_ref[...]


def _mod_call(cc, w_mod, b_mod):
    n = w_mod.shape[1]
    tn = 1024
    return pl.pallas_call(
        _mod_kernel,
        grid=(n // tn,),
        in_specs=[pl.BlockSpec((MOD_ROWS, D_MODEL), lambda j: (0, 0)),
                  pl.BlockSpec((D_MODEL, tn), lambda j: (0, j)),
                  pl.BlockSpec((1, tn), lambda j: (0, j))],
        out_specs=pl.BlockSpec((MOD_ROWS, tn), lambda j: (0, j)),
        out_shape=jax.ShapeDtypeStruct((MOD_ROWS, n), F32),
        compiler_params=_params(("arbitrary",)),
        name="mod",
    )(cc, w_mod, b_mod)


def _store_k(k_ref, kn, kr):
    krb = kr.astype(BF16)
    for h in range(N_HEADS):
        k_ref[0, :, h * HEAD_PAD:h * HEAD_PAD + QK_NOPE] = (
            kn[:, h * QK_NOPE:(h + 1) * QK_NOPE].astype(BF16))
        k_ref[0, :, h * HEAD_PAD + QK_NOPE:(h + 1) * HEAD_PAD] = krb


def _inproj_kernel(x_ref, mod_ref, tabq_ref, tabk_ref, gmix_ref, gq_ref, gkv_ref,
                   wa_ref, wq1t_ref, wq2t_ref, wkn_ref, wvt_ref,
                   u_ref, gate_ref, qt_ref, k_ref, vt_ref):
    m = mod_ref[0]
    shift, scale = m[:, 0:D_MODEL], m[:, D_MODEL:2 * D_MODEL]
    hx = (_rms(x_ref[0], gmix_ref[...]) * (1.0 + scale) + shift).astype(BF16)

    cn = 512
    for j in range(0, D_CONV, cn):
        a = _dot(hx, wa_ref[:, _A0 + j:_A0 + j + cn])
        g = _dot(hx, wa_ref[:, _G0 + j:_G0 + j + cn])
        u_ref[0, :, j:j + cn] = a * jax.nn.sigmoid(g)
    for j in range(0, 2 * D_MODEL, cn):
        gate_ref[0, :, j:j + cn] = jax.nn.sigmoid(
            _dot(hx, wa_ref[:, _GT0 + j:_GT0 + j + cn])).astype(BF16)

    z = _dot(hx, wa_ref[:, _QA0:_WA_COLS])
    qa = z[:, 0:Q_LORA]
    kva = z[:, Q_LORA:Q_LORA + KV_LORA]
    kr_a = z[:, _KR0 - _QA0:_KR0 - _QA0 + LANES]
    kr_b = z[:, _KRS0 - _QA0:_KRS0 - _QA0 + LANES]

    tabq = tabq_ref[...]
    tq_c, tq_s = tabq[0:LANES], tabq[LANES:2 * LANES]
    tabk = tabk_ref[...]
    tk_c, tk_s = tabk[:, 0:LANES], tabk[:, LANES:2 * LANES]

    qn = _rms(qa, gq_ref[...]).astype(BF16)
    qf = _dot_nt(wq1t_ref[...], qn)
    qs = _dot_nt(wq2t_ref[...], qn)
    for h in range(N_HEADS):
        c0 = h * HEAD_PAD
        q_nope = (qf[c0:c0 + QK_NOPE] * Q_SCALE).astype(BF16)
        q_rope = (qf[c0 + QK_NOPE:c0 + HEAD_PAD] * tq_c
                  + qs[h * LANES:(h + 1) * LANES] * tq_s).astype(BF16)
        for jb in range(qt_ref.shape[1]):
            qt_ref[0, jb, c0:c0 + QK_NOPE, :] = q_nope[:, jb * Q_BLK:(jb + 1) * Q_BLK]
            qt_ref[0, jb, c0 + QK_NOPE:c0 + HEAD_PAD, :] = q_rope[:, jb * Q_BLK:(jb + 1) * Q_BLK]

    kvn = _rms(kva, gkv_ref[...]).astype(BF16)
    _store_k(k_ref, _dot(kvn, wkn_ref[...]), kr_a * tk_c + kr_b * tk_s)
    vt_ref[0] = _dot_nt(wvt_ref[...], kvn).astype(BF16)


def _inproj_call(x, mod3, tabq, tabk, g_mix, g_q, g_kv, wa, wq1t, wq2t, wkn, wvt, tm):
    B, S, _ = x.shape
    row = lambda w: pl.BlockSpec((1, tm, w), lambda b, i: (b, i, 0))
    col = lambda w: pl.BlockSpec((1, w, tm), lambda b, i: (b, 0, i))
    return pl.pallas_call(
        _inproj_kernel,
        grid=(B, S // tm),
        in_specs=[row(D_MODEL),
                  pl.BlockSpec((1, 1, 6 * D_MODEL), lambda b, i: (b, 0, 0)),
                  pl.BlockSpec((2 * LANES, tm), lambda b, i: (0, i)),
                  pl.BlockSpec((tm, 2 * LANES), lambda b, i: (i, 0)),
                  _resident((1, D_MODEL)), _resident((1, Q_LORA)), _resident((1, KV_LORA)),
                  _resident(wa.shape), _resident(wq1t.shape), _resident(wq2t.shape),
                  _resident(wkn.shape), _resident(wvt.shape)],
        out_specs=[row(D_CONV), row(2 * D_MODEL),
                   pl.BlockSpec((1, tm // Q_BLK, N_HEADS * HEAD_PAD, Q_BLK), lambda b, i: (b, i, 0, 0)),
                   row(N_HEADS * HEAD_PAD), col(N_HEADS * V_DIM)],
        out_shape=[jax.ShapeDtypeStruct((B, S, D_CONV), F32),
                   jax.ShapeDtypeStruct((B, S, 2 * D_MODEL), BF16),
                   jax.ShapeDtypeStruct((B, S // Q_BLK, N_HEADS * HEAD_PAD, Q_BLK), BF16),
                   jax.ShapeDtypeStruct((B, S, N_HEADS * HEAD_PAD), BF16),
                   jax.ShapeDtypeStruct((B, N_HEADS * V_DIM, S), BF16)],
        compiler_params=_params(("arbitrary", "arbitrary")),
        name="inproj",
    )(x, mod3, tabq, tabk, g_mix, g_q, g_kv, wa, wq1t, wq2t, wkn, wvt)


def _ctxproj_kernel(c_ref, mod_ref, gmix_ref, gkv_ref, wc_ref, wkn_ref, wvt_ref, k_ref, vt_ref):
    m = mod_ref[0]
    shift, scale = m[:, 0:D_MODEL], m[:, D_MODEL:2 * D_MODEL]
    hc = (_rms(c_ref[0], gmix_ref[...]) * (1.0 + scale) + shift).astype(BF16)
    z = _dot(hc, wc_ref[...])
    kvn = _rms(z[:, 0:KV_LORA], gkv_ref[...]).astype(BF16)
    _store_k(k_ref, _dot(kvn, wkn_ref[...]), z[:, KV_LORA:KV_LORA + LANES])
    vt_ref[0] = _dot_nt(wvt_ref[...], kvn).astype(BF16)


def _ctxproj_call(ctx, mod3, g_mix, g_kv, wc, wkn, wvt, ctx_row):
    B, C, _ = ctx.shape
    return pl.pallas_call(
        _ctxproj_kernel,
        grid=(B,),
        in_specs=[pl.BlockSpec((1, C, D_MODEL), lambda b: (b, 0, 0)),
                  pl.BlockSpec((1, 1, 6 * D_MODEL), lambda b: (ctx_row, 0, 0)),
                  _resident((1, D_MODEL)), _resident((1, KV_LORA)),
                  _resident(wc.shape), _resident(wkn.shape), _resident(wvt.shape)],
        out_specs=[pl.BlockSpec((1, C, N_HEADS * HEAD_PAD), lambda b: (b, 0, 0)),
                   pl.BlockSpec((1, N_HEADS * V_DIM, C), lambda b: (b, 0, 0))],
        out_shape=[jax.ShapeDtypeStruct((B, C, N_HEADS * HEAD_PAD), BF16),
                   jax.ShapeDtypeStruct((B, N_HEADS * V_DIM, C), BF16)],
        compiler_params=_params(("arbitrary",)),
        name="ctxproj",
    )(ctx, mod3, g_mix, g_kv, wc, wkn, wvt)


def _attn_kernel(qt_ref, kx_ref, kc_ref, vxt_ref, vct_ref, o_ref, s_ref):
    nq = qt_ref.shape[1]
    n_ctx = kc_ref.shape[1] // K_BLK
    nch = n_ctx + kx_ref.shape[1] // K_BLK

    def k_chunk(c):
        if c < n_ctx:
            return kc_ref[0, c * K_BLK:(c + 1) * K_BLK, :]
        return kx_ref[0, (c - n_ctx) * K_BLK:(c - n_ctx + 1) * K_BLK, :]

    def vt_chunk(c):
        if c < n_ctx:
            return vct_ref[0, :, c * K_BLK:(c + 1) * K_BLK]
        return vxt_ref[0, :, (c - n_ctx) * K_BLK:(c - n_ctx + 1) * K_BLK]

    def fold8(x, op):
        return op(x.reshape(K_BLK // 8, 8, Q_BLK), axis=0)

    items = [(j, c) for j in range(nq) for c in range(nch)]
    lead = nch + ATTN_LAG
    mxs = {}

    def scores(t):
        j, c = items[t]
        s = _dot(k_chunk(c), qt_ref[0, j])
        s_ref[j % 2, c] = s
        r = fold8(s, jnp.max)
        mxs[j] = r if j not in mxs else jnp.maximum(mxs[j], r)

    for t in range(min(lead, len(items))):
        scores(t)
    for g, (j, c) in enumerate(items):
        if c == 0:
            m = jnp.max(mxs.pop(j), axis=0, keepdims=True)
            l8, ot = None, None
        if g + lead < len(items):
            scores(g + lead)
        p = jnp.exp2(s_ref[j % 2, c] - m)
        ps = fold8(p, jnp.sum)
        l8 = ps if l8 is None else l8 + ps
        d = _dot(vt_chunk(c), p.astype(BF16))
        ot = d if ot is None else ot + d
        if c == nch - 1:
            l = jnp.sum(l8, axis=0, keepdims=True)
            o_ref[0, j * Q_BLK:(j + 1) * Q_BLK, :] = (ot * (1.0 / l)).T.astype(o_ref.dtype)


def _attn_call(qt, kx, kc, vxt, vct):
    B, S, _ = kx.shape
    C = kc.shape[1]
    assert C % K_BLK == 0 and S % K_BLK == 0 and S % Q_BLK == 0
    return pl.pallas_call(
        _attn_kernel,
        grid=(B, N_HEADS),
        in_specs=[pl.BlockSpec((1, S // Q_BLK, HEAD_PAD, Q_BLK), lambda b, h: (b, 0, h, 0)),
                  pl.BlockSpec((1, S, HEAD_PAD), lambda b, h: (b, 0, h)),
                  pl.BlockSpec((1, C, HEAD_PAD), lambda b, h: (b, 0, h)),
                  pl.BlockSpec((1, V_DIM, S), lambda b, h: (b, h, 0)),
                  pl.BlockSpec((1, V_DIM, C), lambda b, h: (b, h, 0))],
        out_specs=pl.BlockSpec((1, S, V_DIM), lambda b, h: (b, 0, h)),
        out_shape=jax.ShapeDtypeStruct((B, S, N_HEADS * V_DIM), BF16),
        scratch_shapes=[pltpu.VMEM((2, (C + S) // K_BLK, K_BLK, Q_BLK), F32)],
        compiler_params=_params(("arbitrary", "arbitrary")),
        name="attn",
    )(qt, kx, kc, vxt, vct)


def _conv_kernel(u_ref, w_ref, b_ref, lg_ref, lb_ref, o_ref, pad_ref, y_ref, *, tc, rows):
    S = u_ref.shape[1]
    i = pl.program_id(1)
    sub = D_CONV // LANES

    @pl.when(i == 0)
    def _():
        zeros = jnp.zeros((CONV_HALO * sub, LANES), F32)
        pad_ref[0:CONV_HALO * sub, :] = zeros
        pad_ref[(CONV_HALO + S) * sub:(2 * CONV_HALO + S) * sub, :] = zeros
        for s in range(sub):
            pad_ref[pl.ds(CONV_HALO * sub + s, S, stride=sub), :] = u_ref[0, :, s * LANES:(s + 1) * LANES]

    bias, lg, lb = b_ref[...], lg_ref[...], lb_ref[...]
    off = CONV_HALO - CONV_WIDTH // 2

    def chunk(r, carry):
        r0 = i * tc + r * rows
        acc = jnp.zeros((rows, sub, LANES), F32)
        for k in range(CONV_WIDTH):
            win = pad_ref[pl.ds(pl.multiple_of((r0 + (off + k)) * sub, sub), rows * sub), :]
            acc = acc + win.reshape(rows, sub, LANES) * w_ref[k]
        y_ref[...] = (acc + bias).reshape(rows * sub, LANES)
        y = jnp.concatenate([y_ref[pl.ds(s, rows, stride=sub), :] for s in range(sub)], axis=-1)
        mu = jnp.mean(y, axis=-1, keepdims=True)
        yc = y - mu
        var = jnp.mean(yc * yc, axis=-1, keepdims=True)
        y = yc * lax.rsqrt(var + EPS) * lg + lb
        o_ref[0, pl.ds(pl.multiple_of(r * rows, rows), rows), :] = _silu(y).astype(o_ref.dtype)
        return carry

    lax.fori_loop(0, tc // rows, chunk, 0)


def _conv_call(u, w3, b3, ln_g, ln_b, tc, rows):
    B, S, _ = u.shape
    sub = D_CONV // LANES
    return pl.pallas_call(
        functools.partial(_conv_kernel, tc=tc, rows=rows),
        grid=(B, S // tc),
        in_specs=[pl.BlockSpec((1, S, D_CONV), lambda b, i: (b, 0, 0)),
                  _resident(w3.shape), _resident(b3.shape), _resident((1, D_CONV)),
                  _resident((1, D_CONV))],
        out_specs=pl.BlockSpec((1, tc, D_CONV), lambda b, i: (b, i, 0)),
        out_shape=jax.ShapeDtypeStruct((B, S, D_CONV), BF16),
        scratch_shapes=[pltpu.VMEM(((S + 2 * CONV_HALO) * sub, LANES), F32),
                        pltpu.VMEM((rows * sub, LANES), F32)],
        compiler_params=_params(("arbitrary", "arbitrary")),
        name="conv",
    )(u, w3, b3, ln_g, ln_b)


def _merge_kernel(cv_ref, att_ref, gate_ref, x_ref, mod_ref, wpw_ref, wo_ref, wout_ref, o_ref):
    m = mod_ref[0]
    gate_mix = m[:, 2 * D_MODEL:3 * D_MODEL]
    y_conv = _dot(cv_ref[0], wpw_ref[...])
    y_mla = _dot(att_ref[0], wo_ref[...])
    g = gate_ref[0]
    merged = (g[:, 0:D_MODEL].astype(F32) * y_conv + g[:, D_MODEL:].astype(F32) * y_mla).astype(BF16)
    o_ref[0] = x_ref[0] + gate_mix * _dot(merged, wout_ref[...])


def _merge_call(cv, att, gates, x, mod3, w_pw, w_o, w_out, tm):
    B, S, _ = x.shape
    row = lambda w: pl.BlockSpec((1, tm, w), lambda b, i: (b, i, 0))
    return pl.pallas_call(
        _merge_kernel,
        grid=(B, S // tm),
        in_specs=[row(D_CONV), row(N_HEADS * V_DIM), row(2 * D_MODEL), row(D_MODEL),
                  pl.BlockSpec((1, 1, 6 * D_MODEL), lambda b, i: (b, 0, 0)),
                  _resident(w_pw.shape), _resident(w_o.shape), _resident(w_out.shape)],
        out_specs=row(D_MODEL),
        out_shape=jax.ShapeDtypeStruct((B, S, D_MODEL), F32),
        compiler_params=_params(("arbitrary", "arbitrary")),
        name="merge",
    )(cv, att, gates, x, mod3, w_pw, w_o, w_out)


def _ffn_kernel(x_ref, mod_ref, gffn_ref, gfin_ref, w13_ref, w2_ref, o_ref, h_ref, *, d_ff, cn):
    m = mod_ref[0]
    shift, scale = m[:, 3 * D_MODEL:4 * D_MODEL], m[:, 4 * D_MODEL:5 * D_MODEL]
    gate_ffn = m[:, 5 * D_MODEL:6 * D_MODEL]
    x = x_ref[0]
    h2 = (_rms(x, gffn_ref[...]) * (1.0 + scale) + shift).astype(BF16)
    for j in range(0, d_ff, cn):
        a = _dot(h2, w13_ref[:, j:j + cn])
        b = _dot(h2, w13_ref[:, d_ff + j:d_ff + j + cn])
        h_ref[:, j:j + cn] = (_silu(a) * b).astype(BF16)
    y = x + gate_ffn * _dot(h_ref[...], w2_ref[...])
    o_ref[0] = _rms(y, gfin_ref[...])


def _ffn_call(x1, mod3, g_ffn, g_final, w13, w2, tm, cn):
    B, S, _ = x1.shape
    d_ff = w2.shape[0]
    row = lambda w: pl.BlockSpec((1, tm, w), lambda b, i: (b, i, 0))
    return pl.pallas_call(
        functools.partial(_ffn_kernel, d_ff=d_ff, cn=cn),
        grid=(B, S // tm),
        in_specs=[row(D_MODEL),
                  pl.BlockSpec((1, 1, 6 * D_MODEL), lambda b, i: (b, 0, 0)),
                  _resident((1, D_MODEL)), _resident((1, D_MODEL)),
                  _resident(w13.shape), _resident(w2.shape)],
        out_specs=row(D_MODEL),
        out_shape=jax.ShapeDtypeStruct((B, S, D_MODEL), F32),
        scratch_shapes=[pltpu.VMEM((tm, d_ff), BF16)],
        compiler_params=_params(("arbitrary", "arbitrary")),
        name="ffn",
    )(x1, mod3, g_ffn, g_final, w13, w2)


def _rope_tables(S):
    rows = S // GRID_W
    row = jnp.repeat(jnp.arange(rows, dtype=F32), GRID_W)
    col = jnp.tile(jnp.arange(GRID_W, dtype=F32), rows)
    axis_dim = QK_ROPE // 2
    inv_freq = ROPE_THETA ** (-jnp.arange(0, axis_dim, 2, dtype=F32) / axis_dim)
    ar, ac = row[:, None] * inv_freq, col[:, None] * inv_freq
    cos = jnp.concatenate([jnp.cos(ar), jnp.cos(ar), jnp.cos(ac), jnp.cos(ac)], axis=-1)
    sin = jnp.concatenate([-jnp.sin(ar), jnp.sin(ar), -jnp.sin(ac), jnp.sin(ac)], axis=-1)
    z = jnp.zeros((S, LANES - QK_ROPE), F32)
    tabk = jnp.concatenate([cos, z, sin, z], axis=-1)
    return (Q_SCALE * tabk).T, tabk


_ROPE_SWAP = np.concatenate([np.arange(16, 32), np.arange(0, 16),
                             np.arange(48, 64), np.arange(32, 48)])


def _prep_weights(w_in, w_uq, w_ukv):
    off_q, off_kv = 2 * D_CONV, 2 * D_CONV + Q_LORA
    off_kr, off_gate = off_kv + KV_LORA, off_kv + KV_LORA + QK_ROPE
    w_kr = w_in[:, off_kr:off_gate]
    z64 = jnp.zeros((D_MODEL, LANES - QK_ROPE), w_in.dtype)
    wa = jnp.concatenate([w_in[:, 0:off_q], w_in[:, off_gate:], w_in[:, off_q:off_kr],
                          w_kr, z64, w_kr[:, _ROPE_SWAP], z64], axis=1).astype(BF16)
    wc = jnp.concatenate([w_in[:, off_kv:off_gate], z64], axis=1).astype(BF16)

    uq = w_uq.reshape(Q_LORA, N_HEADS, QK_NOPE + QK_ROPE)
    nope, rope = uq[:, :, :QK_NOPE], uq[:, :, QK_NOPE:]
    zq = jnp.zeros((Q_LORA, N_HEADS, LANES - QK_ROPE), w_uq.dtype)
    wq1t = jnp.concatenate([nope, rope, zq], axis=2).reshape(Q_LORA, N_HEADS * HEAD_PAD).T.astype(BF16)
    wq2t = jnp.concatenate([rope[:, :, _ROPE_SWAP], zq], axis=2).reshape(Q_LORA, N_HEADS * LANES).T.astype(BF16)

    ukv = w_ukv.reshape(KV_LORA, N_HEADS, QK_NOPE + V_DIM)
    wkn = ukv[:, :, :QK_NOPE].reshape(KV_LORA, N_HEADS * QK_NOPE).astype(BF16)
    wvt = ukv[:, :, QK_NOPE:].reshape(KV_LORA, N_HEADS * V_DIM).T.astype(BF16)
    return wa, wc, wq1t, wq2t, wkn, wvt


def kernel(x, c, ctx, c_ctx, w_mod, b_mod, g_mix, g_ffn, w_in, g_q, g_kv, w_uq, w_ukv,
           w_o_mla, w_dw, b_dw, ln_g, ln_b, w_pw, w_out, w_13, w_2, g_final):
    B, S, _ = x.shape
    assert w_mod.shape[0] == 1, "single-layer block"
    assert B + 1 <= MOD_ROWS

    wa, wc, wq1t, wq2t, wkn, wvt = _prep_weights(w_in[0], w_uq[0], w_ukv[0])
    tabq, tabk = _rope_tables(S)
    w3 = w_dw[0].reshape(CONV_WIDTH, D_CONV // LANES, LANES)
    b3 = b_dw.reshape(D_CONV // LANES, LANES)

    cc = jnp.zeros((MOD_ROWS, D_MODEL), F32).at[0:B].set(c).at[B].set(c_ctx)
    mod3 = _mod_call(cc, w_mod[0], b_mod).reshape(MOD_ROWS, 1, 6 * D_MODEL)

    u, gates, qt, kx, vxt = _inproj_call(x, mod3, tabq, tabk, g_mix, g_q, g_kv,
                                         wa, wq1t, wq2t, wkn, wvt, tm=256)
    kc, vct = _ctxproj_call(ctx, mod3, g_mix, g_kv, wc, wkn, wvt, ctx_row=B)
    att = _attn_call(qt, kx, kc, vxt, vct)
    cv = _conv_call(u, w3, b3, ln_g, ln_b, tc=512, rows=32)
    x1 = _merge_call(cv, att, gates, x, mod3, w_pw[0].astype(BF16), w_o_mla[0].astype(BF16),
                     w_out[0].astype(BF16), tm=512)
    return _ffn_call(x1, mod3, g_ffn, g_final.reshape(1, D_MODEL), w_13[0].astype(BF16),
                     w_2[0].astype(BF16), tm=512, cn=256)
```

```python
import functools

import jax
import jax.numpy as jnp
import numpy as np
from jax import lax
from jax.experimental import pallas as pl
from jax.experimental.pallas import tpu as pltpu

F32 = jnp.float32
BF16 = jnp.bfloat16

D_MODEL = 1024
N_HEADS = 8
QK_NOPE = 128
QK_ROPE = 64
V_DIM = 128
Q_LORA = 384
KV_LORA = 256
D_CONV = 1024
CONV_WIDTH = 31
CONV_HALO = 16
GRID_W = 64
ROPE_THETA = 10000.0
EPS = 1e-6
ATTN_SCALE = (QK_NOPE + QK_ROPE) ** -0.5
Q_SCALE = ATTN_SCALE * float(np.log2(np.e))
HEAD_PAD = 256
Q_BLK = 256
K_BLK = 256
ATTN_LAG = 6
CONV_ROWS = 8
CONV_RING = 4
TAIL_LAG = 3
LANES = 128
MOD_ROWS = 24

VMEM_LIMIT = 56 * 1024 * 1024

_A0, _G0, _GT0, _QA0, _KVA0, _KR0, _KRS0, _WA_COLS = 0, 1024, 2048, 4096, 4480, 4736, 4864, 4992

_NT = (((1,), (1,)), ((), ()))


def _resident(shape):
    nd = len(shape)
    return pl.BlockSpec(shape, lambda *_: (0,) * nd, pipeline_mode=pl.Buffered(1))


def _params(sem):
    return pltpu.CompilerParams(dimension_semantics=sem, vmem_limit_bytes=VMEM_LIMIT)


def _rms(x, g):
    return x * lax.rsqrt(jnp.mean(x * x, axis=-1, keepdims=True) + EPS) * g


def _silu(x):
    return x * jax.nn.sigmoid(x)


def _dot(a, b):
    return jnp.dot(a, b, preferred_element_type=F32)


def _dot_nt(a, b):
    return lax.dot_general(a, b, _NT, preferred_element_type=F32)


def _zero_of(x):
    u = pltpu.bitcast(x, jnp.uint32)
    u = lax.shift_right_logical(lax.shift_right_logical(u, jnp.uint32(16)), jnp.uint32(16))
    return pltpu.bitcast(u, F32)


def _mod_kernel(cc_ref, w_ref, b_ref, o_ref):
    s = _silu(cc_ref[...])
    o_ref[...] = jnp.dot(s, w_ref[...], preferred_element_type=F32,
                         precision=lax.Precision.HIGHEST) + b_ref[...]


def _mod_call(cc, w_mod, b_mod):
    n = w_mod.shape[1]
    tn = 1024
    return pl.pallas_call(
        _mod_kernel,
        grid=(n // tn,),
        in_specs=[pl.BlockSpec((MOD_ROWS, D_MODEL), lambda j: (0, 0)),
                  pl.BlockSpec((D_MODEL, tn), lambda j: (0, j)),
                  pl.BlockSpec((1, tn), lambda j: (0, j))],
        out_specs=pl.BlockSpec((MOD_ROWS, tn), lambda j: (0, j)),
        out_shape=jax.ShapeDtypeStruct((MOD_ROWS, n), F32),
        compiler_params=_params(("arbitrary",)),
        name="mod",
    )(cc, w_mod, b_mod)


def _store_k(k_ref, kn, kr):
    krb = kr.astype(BF16)
    for h in range(N_HEADS):
        k_ref[0, :, h * HEAD_PAD:h * HEAD_PAD + QK_NOPE] = (
            kn[:, h * QK_NOPE:(h + 1) * QK_NOPE].astype(BF16))
        k_ref[0, :, h * HEAD_PAD + QK_NOPE:(h + 1) * HEAD_PAD] = krb


def _inproj_kernel(x_ref, mod_ref, tabq_ref, tabk_ref, gmix_ref, gq_ref, gkv_ref,
                   wa_ref, wq1t_ref, wq2t_ref, wkn_ref, wvt_ref,
                   u_ref, gate_ref, qt_ref, k_ref, vt_ref):
    m = mod_ref[0]
    shift, scale = m[:, 0:D_MODEL], m[:, D_MODEL:2 * D_MODEL]
    hx = (_rms(x_ref[0], gmix_ref[...]) * (1.0 + scale) + shift).astype(BF16)

    cn = 512
    for j in range(0, D_CONV, cn):
        a = _dot(hx, wa_ref[:, _A0 + j:_A0 + j + cn])
        g = _dot(hx, wa_ref[:, _G0 + j:_G0 + j + cn])
        u_ref[0, :, j:j + cn] = a * jax.nn.sigmoid(g)
    for j in range(0, 2 * D_MODEL, cn):
        gate_ref[0, :, j:j + cn] = jax.nn.sigmoid(
            _dot(hx, wa_ref[:, _GT0 + j:_GT0 + j + cn])).astype(BF16)

    z = _dot(hx, wa_ref[:, _QA0:_WA_COLS])
    qa = z[:, 0:Q_LORA]
    kva = z[:, Q_LORA:Q_LORA + KV_LORA]
    kr_a = z[:, _KR0 - _QA0:_KR0 - _QA0 + LANES]
    kr_b = z[:, _KRS0 - _QA0:_KRS0 - _QA0 + LANES]

    tabq = tabq_ref[...]
    tq_c, tq_s = tabq[0:LANES], tabq[LANES:2 * LANES]
    tabk = tabk_ref[...]
    tk_c, tk_s = tabk[:, 0:LANES], tabk[:, LANES:2 * LANES]

    qn = _rms(qa, gq_ref[...]).astype(BF16)
    qf = _dot_nt(wq1t_ref[...], qn)
    qs = _dot_nt(wq2t_ref[...], qn)
    for h in range(N_HEADS):
        c0 = h * HEAD_PAD
        q_nope = (qf[c0:c0 + QK_NOPE] * Q_SCALE).astype(BF16)
        q_rope = (qf[c0 + QK_NOPE:c0 + HEAD_PAD] * tq_c
                  + qs[h * LANES:(h + 1) * LANES] * tq_s).astype(BF16)
        for jb in range(qt_ref.shape[1]):
            qt_ref[0, jb, c0:c0 + QK_NOPE, :] = q_nope[:, jb * Q_BLK:(jb + 1) * Q_BLK]
            qt_ref[0, jb, c0 + QK_NOPE:c0 + HEAD_PAD, :] = q_rope[:, jb * Q_BLK:(jb + 1) * Q_BLK]

    kvn = _rms(kva, gkv_ref[...]).astype(BF16)
    _store_k(k_ref, _dot(kvn, wkn_ref[...]), kr_a * tk_c + kr_b * tk_s)
    vt_ref[0] = _dot_nt(wvt_ref[...], kvn).astype(BF16)


def _inproj_call(x, mod3, tabq, tabk, g_mix, g_q, g_kv, wa, wq1t, wq2t, wkn, wvt, tm):
    B, S, _ = x.shape
    row = lambda w: pl.BlockSpec((1, tm, w), lambda b, i: (b, i, 0))
    col = lambda w: pl.BlockSpec((1, w, tm), lambda b, i: (b, 0, i))
    return pl.pallas_call(
        _inproj_kernel,
        grid=(B, S // tm),
        in_specs=[row(D_MODEL),
                  pl.BlockSpec((1, 1, 6 * D_MODEL), lambda b, i: (b, 0, 0)),
                  pl.BlockSpec((2 * LANES, tm), lambda b, i: (0, i)),
                  pl.BlockSpec((tm, 2 * LANES), lambda b, i: (i, 0)),
                  _resident((1, D_MODEL)), _resident((1, Q_LORA)), _resident((1, KV_LORA)),
                  _resident(wa.shape), _resident(wq1t.shape), _resident(wq2t.shape),
                  _resident(wkn.shape), _resident(wvt.shape)],
        out_specs=[row(D_CONV), row(2 * D_MODEL),
                   pl.BlockSpec((1, tm // Q_BLK, N_HEADS * HEAD_PAD, Q_BLK), lambda b, i: (b, i, 0, 0)),
                   row(N_HEADS * HEAD_PAD), col(N_HEADS * V_DIM)],
        out_shape=[jax.ShapeDtypeStruct((B, S, D_CONV), F32),
                   jax.ShapeDtypeStruct((B, S, 2 * D_MODEL), BF16),
                   jax.ShapeDtypeStruct((B, S // Q_BLK, N_HEADS * HEAD_PAD, Q_BLK), BF16),
                   jax.ShapeDtypeStruct((B, S, N_HEADS * HEAD_PAD), BF16),
                   jax.ShapeDtypeStruct((B, N_HEADS * V_DIM, S), BF16)],
        compiler_params=_params(("arbitrary", "arbitrary")),
        name="inproj",
    )(x, mod3, tabq, tabk, g_mix, g_q, g_kv, wa, wq1t, wq2t, wkn, wvt)


def _ctxproj_kernel(c_ref, mod_ref, gmix_ref, gkv_ref, wc_ref, wkn_ref, wvt_ref, k_ref, vt_ref):
    m = mod_ref[0]
    shift, scale = m[:, 0:D_MODEL], m[:, D_MODEL:2 * D_MODEL]
    hc = (_rms(c_ref[0], gmix_ref[...]) * (1.0 + scale) + shift).astype(BF16)
    z = _dot(hc, wc_ref[...])
    kvn = _rms(z[:, 0:KV_LORA], gkv_ref[...]).astype(BF16)
    _store_k(k_ref, _dot(kvn, wkn_ref[...]), z[:, KV_LORA:KV_LORA + LANES])
    vt_ref[0] = _dot_nt(wvt_ref[...], kvn).astype(BF16)


def _ctxproj_call(ctx, mod3, g_mix, g_kv, wc, wkn, wvt, ctx_row):
    B, C, _ = ctx.shape
    return pl.pallas_call(
        _ctxproj_kernel,
        grid=(B,),
        in_specs=[pl.BlockSpec((1, C, D_MODEL), lambda b: (b, 0, 0)),
                  pl.BlockSpec((1, 1, 6 * D_MODEL), lambda b: (ctx_row, 0, 0)),
                  _resident((1, D_MODEL)), _resident((1, KV_LORA)),
                  _resident(wc.shape), _resident(wkn.shape), _resident(wvt.shape)],
        out_specs=[pl.BlockSpec((1, C, N_HEADS * HEAD_PAD), lambda b: (b, 0, 0)),
                   pl.BlockSpec((1, N_HEADS * V_DIM, C), lambda b: (b, 0, 0))],
        out_shape=[jax.ShapeDtypeStruct((B, C, N_HEADS * HEAD_PAD), BF16),
                   jax.ShapeDtypeStruct((B, N_HEADS * V_DIM, C), BF16)],
        compiler_params=_params(("arbitrary",)),
        name="ctxproj",
    )(ctx, mod3, g_mix, g_kv, wc, wkn, wvt)


def _attn_kernel(qt_ref, kx_ref, kc_ref, vxt_ref, vct_ref, o_ref, s_ref):
    nq = qt_ref.shape[1]
    n_ctx = kc_ref.shape[1] // K_BLK
    nch = n_ctx + kx_ref.shape[1] // K_BLK

    def k_chunk(c):
        if c < n_ctx:
            return kc_ref[0, c * K_BLK:(c + 1) * K_BLK, :]
        return kx_ref[0, (c - n_ctx) * K_BLK:(c - n_ctx + 1) * K_BLK, :]

    def vt_chunk(c):
        if c < n_ctx:
            return vct_ref[0, :, c * K_BLK:(c + 1) * K_BLK]
        return vxt_ref[0, :, (c - n_ctx) * K_BLK:(c - n_ctx + 1) * K_BLK]

    def fold8(x, op):
        return op(x.reshape(K_BLK // 8, 8, Q_BLK), axis=0)

    items = [(j, c) for j in range(nq) for c in range(nch)]
    lead = nch + ATTN_LAG
    mxs = {}

    def scores(t):
        j, c = items[t]
        s = _dot(k_chunk(c), qt_ref[0, j])
        s_ref[j % 2, c] = s
        r = fold8(s, jnp.max)
        mxs[j] = r if j not in mxs else jnp.maximum(mxs[j], r)

    for t in range(min(lead, len(items))):
        scores(t)
    for g, (j, c) in enumerate(items):
        if c == 0:
            m = jnp.max(mxs.pop(j), axis=0, keepdims=True)
            l8, ot = None, None
        if g + lead < len(items):
            scores(g + lead)
        p = jnp.exp2(s_ref[j % 2, c] - m)
        ps = fold8(p, jnp.sum)
        l8 = ps if l8 is None else l8 + ps
        d = _dot(vt_chunk(c), p.astype(BF16))
        ot = d if ot is None else ot + d
        if c == nch - 1:
            l = jnp.sum(l8, axis=0, keepdims=True)
            o_ref[0, j * Q_BLK:(j + 1) * Q_BLK, :] = (ot * (1.0 / l)).T.astype(o_ref.dtype)


def _attn_call(qt, kx, kc, vxt, vct):
    B, S, _ = kx.shape
    C = kc.shape[1]
    assert C % K_BLK == 0 and S % K_BLK == 0 and S % Q_BLK == 0
    return pl.pallas_call(
        _attn_kernel,
        grid=(B, N_HEADS),
        in_specs=[pl.BlockSpec((1, S // Q_BLK, HEAD_PAD, Q_BLK), lambda b, h: (b, 0, h, 0)),
                  pl.BlockSpec((1, S, HEAD_PAD), lambda b, h: (b, 0, h)),
                  pl.BlockSpec((1, C, HEAD_PAD), lambda b, h: (b, 0, h)),
                  pl.BlockSpec((1, V_DIM, S), lambda b, h: (b, h, 0)),
                  pl.BlockSpec((1, V_DIM, C), lambda b, h: (b, h, 0))],
        out_specs=pl.BlockSpec((1, S, V_DIM), lambda b, h: (b, 0, h)),
        out_shape=jax.ShapeDtypeStruct((B, S, N_HEADS * V_DIM), BF16),
        scratch_shapes=[pltpu.VMEM((2, (C + S) // K_BLK, K_BLK, Q_BLK), F32)],
        compiler_params=_params(("arbitrary", "arbitrary")),
        name="attn",
    )(qt, kx, kc, vxt, vct)


def _tail_kernel(u_ref, att_ref, gate_ref, x_ref, mod_ref, w3_ref, b3_ref, lg_ref, lb_ref,
                 gffn_ref, gfin_ref, wpw_ref, wo_ref, wout_ref, w13_ref, w2_ref,
                 o_ref, ring_ref, y_ref, cv_ref, mg_ref, x1_ref, h2_ref, h_ref,
                 *, n_seq, n_tiles, cn):
    tm = x_ref.shape[1]
    d_ff = w2_ref.shape[0]
    sub = D_CONV // LANES
    g = pl.program_id(0)

    @pl.when(g == 0)
    def _():
        ring_ref[...] = jnp.zeros(ring_ref.shape, F32)
        cv_ref[...] = jnp.zeros(cv_ref.shape, BF16)

    def ingest():
        pos = lax.rem(g, n_seq)
        keep_prev = jnp.where(jnp.logical_or(pos == 0, g >= n_tiles), 0.0, 1.0).astype(F32)
        keep_next = jnp.where(pos == n_seq - 1, 0.0, 1.0).astype(F32)
        slot, nxt = lax.rem(g, CONV_RING), lax.rem(g + 1, CONV_RING)
        prv = lax.rem(g + CONV_RING - 1, CONV_RING)
        for s in range(sub):
            col = u_ref[0, :, s * LANES:(s + 1) * LANES]
            ring_ref[slot, pl.ds(CONV_HALO * sub + s, tm, stride=sub), :] = col
            ring_ref[prv, pl.ds((CONV_HALO + tm) * sub + s, CONV_HALO, stride=sub), :] = (
                col[0:CONV_HALO] * keep_prev)
            ring_ref[nxt, pl.ds(s, CONV_HALO, stride=sub), :] = col[tm - CONV_HALO:tm] * keep_next

    cslot = lax.rem(g + 2, CONV_RING)
    cpar = lax.rem(g, 2)
    bias, lg, lb = b3_ref[...], lg_ref[...], lb_ref[...]
    off = CONV_HALO - CONV_WIDTH // 2

    def conv_chunk(r, gate):
        acc = jnp.zeros((CONV_ROWS, sub, LANES), F32)
        for k in range(CONV_WIDTH):
            t0 = (r * CONV_ROWS + off + k) * sub
            win = ring_ref[cslot, t0:t0 + CONV_ROWS * sub, :]
            wk = w3_ref[k] if (k > 0 or gate is None) else w3_ref[k] + gate
            acc = acc + win.reshape(CONV_ROWS, sub, LANES) * wk
        y_ref[r] = (acc + bias).reshape(CONV_ROWS * sub, LANES)
        y = jnp.concatenate([y_ref[r, pl.ds(s, CONV_ROWS, stride=sub), :] for s in range(sub)], axis=-1)
        mu = jnp.mean(y, axis=-1, keepdims=True)
        yc = y - mu
        var = jnp.mean(yc * yc, axis=-1, keepdims=True)
        y = yc * lax.rsqrt(var + EPS) * lg + lb
        out = _silu(y)
        cv_ref[cpar, r * CONV_ROWS:(r + 1) * CONV_ROWS, :] = out.astype(BF16)
        return _zero_of(out[0:8, 0:LANES])

    m = mod_ref[0]
    gate_mix = m[:, 2 * D_MODEL:3 * D_MODEL]
    shift, scale = m[:, 3 * D_MODEL:4 * D_MODEL], m[:, 4 * D_MODEL:5 * D_MODEL]
    gate_ffn = m[:, 5 * D_MODEL:6 * D_MODEL]
    st = {}

    def anchor(ref, top, tok):
        if tok is not None:
            ref[0:16, 0:LANES] = (top + jnp.concatenate([tok, tok], axis=0)).astype(BF16)

    def proj_conv(tok):
        del tok
        st["y_conv"] = _dot(cv_ref[1 - cpar], wpw_ref[...])
        return _zero_of(st["y_conv"][0:8, 0:LANES])

    def proj_mla(tok):
        del tok
        y_mla = _dot(att_ref[0], wo_ref[...])
        gt = gate_ref[0]
        merged = gt[:, 0:D_MODEL].astype(F32) * st.pop("y_conv") + gt[:, D_MODEL:].astype(F32) * y_mla
        st["mg_top"] = merged[0:16, 0:LANES]
        mg_ref[...] = merged.astype(BF16)
        return _zero_of(y_mla[0:8, 0:LANES])

    def proj_out(tok):
        anchor(mg_ref, st["mg_top"], tok)
        yo = _dot(mg_ref[...], wout_ref[...])
        x1_ref[...] = x_ref[0] + gate_mix * yo
        h2 = _rms(x1_ref[...], gffn_ref[...]) * (1.0 + scale) + shift
        st["h2_top"] = h2[0:16, 0:LANES]
        h2_ref[...] = h2.astype(BF16)
        return _zero_of(yo[0:8, 0:LANES])

    def ffn_chunk(j, tok):
        anchor(h2_ref, st["h2_top"], tok)
        h2 = h2_ref[...]
        a = _dot(h2, w13_ref[:, j:j + cn])
        b = _dot(h2, w13_ref[:, d_ff + j:d_ff + j + cn])
        hv = _silu(a) * b
        if j == 0:
            st["h_top"] = hv[0:16, 0:LANES]
        h_ref[:, j:j + cn] = hv.astype(BF16)
        return _zero_of(a[0:8, 0:LANES])

    def down_chunk(j, tok):
        anchor(h_ref, st["h_top"], tok)
        yd = _dot(h_ref[...], w2_ref[:, j:j + cn])
        x1_ref[:, j:j + cn] = x1_ref[:, j:j + cn] + gate_ffn[:, j:j + cn] * yd
        return _zero_of(yd[0:8, 0:LANES])

    mxu_tasks = ([proj_conv, proj_mla, proj_out]
                 + [functools.partial(ffn_chunk, j) for j in range(0, d_ff, cn)]
                 + [functools.partial(down_chunk, j) for j in range(0, D_MODEL, cn)])
    cost = ([0] + [D_MODEL * D_MODEL] * 2 + [2 * D_MODEL * cn] * (d_ff // cn)
            + [d_ff * cn] * (D_MODEL // cn))
    n_conv = tm // CONV_ROWS
    done = 0
    mxu_toks, group_toks = [None] * TAIL_LAG, [None] * TAIL_LAG
    for i, task in enumerate(mxu_tasks):
        want = (sum(cost[:i + 1]) * n_conv) // sum(cost)
        group = None
        for r in range(done, want):
            tok = conv_chunk(r, mxu_toks[-TAIL_LAG])
            group = tok if group is None else group + tok
        done = want
        mxu_toks.append(task(group_toks[-TAIL_LAG]))
        group_toks.append(group)
    o_ref[0] = _rms(x1_ref[...], gfin_ref[...])
    ingest()


def _tail_call(u, att, gates, x, mod3, w3, b3, ln_g, ln_b, g_ffn, g_final,
               w_pw, w_o, w_out, w13, w2, tm, cn):
    B, S, _ = x.shape
    n_seq = S // tm
    n_tiles = B * n_seq
    d_ff = w2.shape[0]
    sub = D_CONV // LANES
    assert tm % CONV_ROWS == 0 and d_ff % cn == 0

    def tile(t):
        return (t // n_seq, lax.rem(t, n_seq), 0)

    in_tile = lambda g: tile(jnp.minimum(g, n_tiles - 1))
    out_t = lambda g: jnp.clip(g - 3, 0, n_tiles - 1)
    out_tile = lambda g: tile(out_t(g))
    row = lambda w, f: pl.BlockSpec((1, tm, w), f)
    return pl.pallas_call(
        functools.partial(_tail_kernel, n_seq=n_seq, n_tiles=n_tiles, cn=cn),
        grid=(n_tiles + 3,),
        in_specs=[row(D_CONV, in_tile), row(N_HEADS * V_DIM, out_tile), row(2 * D_MODEL, out_tile),
                  row(D_MODEL, out_tile),
                  pl.BlockSpec((1, 1, 6 * D_MODEL), lambda g: (out_t(g) // n_seq, 0, 0)),
                  _resident(w3.shape), _resident(b3.shape), _resident((1, D_CONV)),
                  _resident((1, D_CONV)), _resident((1, D_MODEL)), _resident((1, D_MODEL)),
                  _resident(w_pw.shape), _resident(w_o.shape), _resident(w_out.shape),
                  _resident(w13.shape), _resident(w2.shape)],
        out_specs=row(D_MODEL, out_tile),
        out_shape=jax.ShapeDtypeStruct((B, S, D_MODEL), F32),
        scratch_shapes=[pltpu.VMEM((CONV_RING, (tm + 2 * CONV_HALO) * sub, LANES), F32),
                        pltpu.VMEM((tm // CONV_ROWS, CONV_ROWS * sub, LANES), F32),
                        pltpu.VMEM((2, tm, D_CONV), BF16),
                        pltpu.VMEM((tm, D_MODEL), BF16),
                        pltpu.VMEM((tm, D_MODEL), F32),
                        pltpu.VMEM((tm, D_MODEL), BF16),
                        pltpu.VMEM((tm, d_ff), BF16)],
        compiler_params=_params(("arbitrary",)),
        name="tail",
    )(u, att, gates, x, mod3, w3, b3, ln_g, ln_b, g_ffn, g_final, w_pw, w_o, w_out, w13, w2)


def _rope_tables(S):
    rows = S // GRID_W
    row = jnp.repeat(jnp.arange(rows, dtype=F32), GRID_W)
    col = jnp.tile(jnp.arange(GRID_W, dtype=F32), rows)
    axis_dim = QK_ROPE // 2
    inv_freq = ROPE_THETA ** (-jnp.arange(0, axis_dim, 2, dtype=F32) / axis_dim)
    ar, ac = row[:, None] * inv_freq, col[:, None] * inv_freq
    cos = jnp.concatenate([jnp.cos(ar), jnp.cos(ar), jnp.cos(ac), jnp.cos(ac)], axis=-1)
    sin = jnp.concatenate([-jnp.sin(ar), jnp.sin(ar), -jnp.sin(ac), jnp.sin(ac)], axis=-1)
    z = jnp.zeros((S, LANES - QK_ROPE), F32)
    tabk = jnp.concatenate([cos, z, sin, z], axis=-1)
    return (Q_SCALE * tabk).T, tabk


_ROPE_SWAP = np.concatenate([np.arange(16, 32), np.arange(0, 16),
                             np.arange(48, 64), np.arange(32, 48)])


def _prep_weights(w_in, w_uq, w_ukv):
    off_q, off_kv = 2 * D_CONV, 2 * D_CONV + Q_LORA
    off_kr, off_gate = off_kv + KV_LORA, off_kv + KV_LORA + QK_ROPE
    w_kr = w_in[:, off_kr:off_gate]
    z64 = jnp.zeros((D_MODEL, LANES - QK_ROPE), w_in.dtype)
    wa = jnp.concatenate([w_in[:, 0:off_q], w_in[:, off_gate:], w_in[:, off_q:off_kr],
                          w_kr, z64, w_kr[:, _ROPE_SWAP], z64], axis=1).astype(BF16)
    wc = jnp.concatenate([w_in[:, off_kv:off_gate], z64], axis=1).astype(BF16)

    uq = w_uq.reshape(Q_LORA, N_HEADS, QK_NOPE + QK_ROPE)
    nope, rope = uq[:, :, :QK_NOPE], uq[:, :, QK_NOPE:]
    zq = jnp.zeros((Q_LORA, N_HEADS, LANES - QK_ROPE), w_uq.dtype)
    wq1t = jnp.concatenate([nope, rope, zq], axis=2).reshape(Q_LORA, N_HEADS * HEAD_PAD).T.astype(BF16)
    wq2t = jnp.concatenate([rope[:, :, _ROPE_SWAP], zq], axis=2).reshape(Q_LORA, N_HEADS * LANES).T.astype(BF16)

    ukv = w_ukv.reshape(KV_LORA, N_HEADS, QK_NOPE + V_DIM)
    wkn = ukv[:, :, :QK_NOPE].reshape(KV_LORA, N_HEADS * QK_NOPE).astype(BF16)
    wvt = ukv[:, :, QK_NOPE:].reshape(KV_LORA, N_HEADS * V_DIM).T.astype(BF16)
    return wa, wc, wq1t, wq2t, wkn, wvt


def kernel(x, c, ctx, c_ctx, w_mod, b_mod, g_mix, g_ffn, w_in, g_q, g_kv, w_uq, w_ukv,
           w_o_mla, w_dw, b_dw, ln_g, ln_b, w_pw, w_out, w_13, w_2, g_final):
    B, S, _ = x.shape
    assert w_mod.shape[0] == 1, "single-layer block"
    assert B + 1 <= MOD_ROWS

    wa, wc, wq1t, wq2t, wkn, wvt = _prep_weights(w_in[0], w_uq[0], w_ukv[0])
    tabq, tabk = _rope_tables(S)
    w3 = w_dw[0].reshape(CONV_WIDTH, D_CONV // LANES, LANES)
    b3 = b_dw.reshape(D_CONV // LANES, LANES)

    cc = jnp.zeros((MOD_ROWS, D_MODEL), F32).at[0:B].set(c).at[B].set(c_ctx)
    mod3 = _mod_call(cc, w_mod[0], b_mod).reshape(MOD_ROWS, 1, 6 * D_MODEL)

    u, gates, qt, kx, vxt = _inproj_call(x, mod3, tabq, tabk, g_mix, g_q, g_kv,
                                         wa, wq1t, wq2t, wkn, wvt, tm=256)
    kc, vct = _ctxproj_call(ctx, mod3, g_mix, g_kv, wc, wkn, wvt, ctx_row=B)
    att = _attn_call(qt, kx, kc, vxt, vct)
    return _tail_call(u, att, gates, x, mod3, w3, b3, ln_g, ln_b, g_ffn, g_final.reshape(1, D_MODEL),
                      w_pw[0].astype(BF16), w_o_mla[0].astype(BF16), w_out[0].astype(BF16),
                      w_13[0].astype(BF16), w_2[0].astype(BF16), tm=256, cn=256)
```

```python
import functools

import jax
import jax.numpy as jnp
import numpy as np
from jax import lax
from jax.experimental import pallas as pl
from jax.experimental.pallas import tpu as pltpu

F32 = jnp.float32
BF16 = jnp.bfloat16

D_MODEL = 1024
N_HEADS = 8
QK_NOPE = 128
QK_ROPE = 64
V_DIM = 128
Q_LORA = 384
KV_LORA = 256
D_CONV = 1024
CONV_WIDTH = 31
CONV_HALO = 16
GRID_W = 64
ROPE_THETA = 10000.0
EPS = 1e-6
ATTN_SCALE = (QK_NOPE + QK_ROPE) ** -0.5
Q_SCALE = ATTN_SCALE * float(np.log2(np.e))
HEAD_PAD = 256
Q_BLK = 256
K_BLK = 256
ATTN_LAG = 6
ATTN_HEADS = 2
CONV_ROWS = 8
CONV_RING = 4
TAIL_LAG = 3
LANES = 128
MOD_ROWS = 24

VMEM_LIMIT = 56 * 1024 * 1024

_KR0 = Q_LORA + KV_LORA
_KRS0 = _KR0 + 128

_NT = (((1,), (1,)), ((), ()))


def _resident(shape):
    nd = len(shape)
    return pl.BlockSpec(shape, lambda *_: (0,) * nd, pipeline_mode=pl.Buffered(1))


def _params(sem):
    return pltpu.CompilerParams(dimension_semantics=sem, vmem_limit_bytes=VMEM_LIMIT)


def _rms(x, g):
    return x * lax.rsqrt(jnp.mean(x * x, axis=-1, keepdims=True) + EPS) * g


def _silu(x):
    return x * jax.nn.sigmoid(x)


def _dot(a, b):
    return jnp.dot(a, b, preferred_element_type=F32)


def _dot_nt(a, b):
    return lax.dot_general(a, b, _NT, preferred_element_type=F32)


def _zero_of(x):
    u = pltpu.bitcast(x, jnp.uint32)
    u = lax.shift_right_logical(lax.shift_right_logical(u, jnp.uint32(16)), jnp.uint32(16))
    return pltpu.bitcast(u, F32)


def _mod_kernel(cc_ref, w_ref, b_ref, o_ref):
    s = _silu(cc_ref[...])
    o_ref[...] = jnp.dot(s, w_ref[...], preferred_element_type=F32,
                         precision=lax.Precision.HIGHEST) + b_ref[...]


def _mod_call(cc, w_mod, b_mod):
    n = w_mod.shape[1]
    tn = 1024
    return pl.pallas_call(
        _mod_kernel,
        grid=(n // tn,),
        in_specs=[pl.BlockSpec((MOD_ROWS, D_MODEL), lambda j: (0, 0)),
                  pl.BlockSpec((D_MODEL, tn), lambda j: (0, j)),
                  pl.BlockSpec((1, tn), lambda j: (0, j))],
        out_specs=pl.BlockSpec((MOD_ROWS, tn), lambda j: (0, j)),
        out_shape=jax.ShapeDtypeStruct((MOD_ROWS, n), F32),
        compiler_params=_params(("arbitrary",)),
        name="mod",
    )(cc, w_mod, b_mod)


def _store_k(k_ref, kn, kr):
    krb = kr.astype(BF16)
    for h in range(N_HEADS):
        k_ref[0, :, h * HEAD_PAD:h * HEAD_PAD + QK_NOPE] = (
            kn[:, h * QK_NOPE:(h + 1) * QK_NOPE].astype(BF16))
        k_ref[0, :, h * HEAD_PAD + QK_NOPE:(h + 1) * HEAD_PAD] = krb


def _inproj_kernel(x_ref, mod_ref, tabq_ref, tabk_ref, gmix_ref, gq_ref, gkv_ref,
                   wglu_ref, wgate_ref, wlat_ref, wq1t_ref, wkn_ref, wvt_ref,
                   u_ref, gate_ref, qt_ref, k_ref, vt_ref):
    m = mod_ref[0]
    shift, scale = m[:, 0:D_MODEL], m[:, D_MODEL:2 * D_MODEL]
    hx = (_rms(x_ref[0], gmix_ref[...]) * (1.0 + scale) + shift).astype(BF16)

    cn = 512
    for j in range(0, D_CONV, cn):
        a = _dot(hx, wglu_ref[:, j:j + cn])
        g = _dot(hx, wglu_ref[:, D_CONV + j:D_CONV + j + cn])
        u_ref[0, :, j:j + cn] = a * jax.nn.sigmoid(g)
    for j in range(0, 2 * D_MODEL, cn):
        gate_ref[0, :, j:j + cn] = jax.nn.sigmoid(_dot(hx, wgate_ref[:, j:j + cn])).astype(BF16)

    z = _dot(hx, wlat_ref[...])
    qa = z[:, 0:Q_LORA]
    kva = z[:, Q_LORA:Q_LORA + KV_LORA]
    kr_a = z[:, _KR0:_KR0 + LANES]
    kr_b = z[:, _KRS0:_KRS0 + LANES]

    tabq = tabq_ref[...]
    tq_c, tq_s = tabq[0:LANES], tabq[LANES:2 * LANES]
    tabk = tabk_ref[...]
    tk_c, tk_s = tabk[:, 0:LANES], tabk[:, LANES:2 * LANES]

    qn = _rms(qa, gq_ref[...]).astype(BF16)
    qf = _dot_nt(wq1t_ref[...], qn)
    half = QK_ROPE // 4
    for h in range(N_HEADS):
        c0 = h * HEAD_PAD
        q_nope = (qf[c0:c0 + QK_NOPE] * Q_SCALE).astype(BF16)
        qr = qf[c0 + QK_NOPE:c0 + HEAD_PAD]
        qr_swap = jnp.concatenate([qr[half:2 * half], qr[0:half], qr[3 * half:4 * half],
                                   qr[2 * half:3 * half], qr[4 * half:]], axis=0)
        q_rope = (qr * tq_c + qr_swap * tq_s).astype(BF16)
        for jb in range(qt_ref.shape[1]):
            qt_ref[0, jb, c0:c0 + QK_NOPE, :] = q_nope[:, jb * Q_BLK:(jb + 1) * Q_BLK]
            qt_ref[0, jb, c0 + QK_NOPE:c0 + HEAD_PAD, :] = q_rope[:, jb * Q_BLK:(jb + 1) * Q_BLK]

    kvn = _rms(kva, gkv_ref[...]).astype(BF16)
    _store_k(k_ref, _dot(kvn, wkn_ref[...]), kr_a * tk_c + kr_b * tk_s)
    vt_ref[0] = _dot_nt(wvt_ref[...], kvn).astype(BF16)


def _inproj_call(x, mod3, tabq, tabk, g_mix, g_q, g_kv, wglu, wgate, wlat, wq1t, wkn, wvt, tm):
    B, S, _ = x.shape
    row = lambda w: pl.BlockSpec((1, tm, w), lambda b, i: (b, i, 0))
    col = lambda w: pl.BlockSpec((1, w, tm), lambda b, i: (b, 0, i))
    return pl.pallas_call(
        _inproj_kernel,
        grid=(B, S // tm),
        in_specs=[row(D_MODEL),
                  pl.BlockSpec((1, 1, 6 * D_MODEL), lambda b, i: (b, 0, 0)),
                  pl.BlockSpec((2 * LANES, tm), lambda b, i: (0, i)),
                  pl.BlockSpec((tm, 2 * LANES), lambda b, i: (i, 0)),
                  _resident((1, D_MODEL)), _resident((1, Q_LORA)), _resident((1, KV_LORA)),
                  _resident(wglu.shape), _resident(wgate.shape), _resident(wlat.shape),
                  _resident(wq1t.shape),
                  _resident(wkn.shape), _resident(wvt.shape)],
        out_specs=[row(D_CONV), row(2 * D_MODEL),
                   pl.BlockSpec((1, tm // Q_BLK, N_HEADS * HEAD_PAD, Q_BLK), lambda b, i: (b, i, 0, 0)),
                   row(N_HEADS * HEAD_PAD), col(N_HEADS * V_DIM)],
        out_shape=[jax.ShapeDtypeStruct((B, S, D_CONV), F32),
                   jax.ShapeDtypeStruct((B, S, 2 * D_MODEL), BF16),
                   jax.ShapeDtypeStruct((B, S // Q_BLK, N_HEADS * HEAD_PAD, Q_BLK), BF16),
                   jax.ShapeDtypeStruct((B, S, N_HEADS * HEAD_PAD), BF16),
                   jax.ShapeDtypeStruct((B, N_HEADS * V_DIM, S), BF16)],
        compiler_params=_params(("arbitrary", "arbitrary")),
        name="inproj",
    )(x, mod3, tabq, tabk, g_mix, g_q, g_kv, wglu, wgate, wlat, wq1t, wkn, wvt)


def _ctxproj_kernel(c_ref, mod_ref, gmix_ref, gkv_ref, wc_ref, wkn_ref, wvt_ref, k_ref, vt_ref):
    m = mod_ref[0]
    shift, scale = m[:, 0:D_MODEL], m[:, D_MODEL:2 * D_MODEL]
    hc = (_rms(c_ref[0], gmix_ref[...]) * (1.0 + scale) + shift).astype(BF16)
    z = _dot(hc, wc_ref[...])
    kvn = _rms(z[:, 0:KV_LORA], gkv_ref[...]).astype(BF16)
    _store_k(k_ref, _dot(kvn, wkn_ref[...]), z[:, KV_LORA:KV_LORA + LANES])
    vt_ref[0] = _dot_nt(wvt_ref[...], kvn).astype(BF16)


def _ctxproj_call(ctx, mod3, g_mix, g_kv, wc, wkn, wvt, ctx_row):
    B, C, _ = ctx.shape
    return pl.pallas_call(
        _ctxproj_kernel,
        grid=(B,),
        in_specs=[pl.BlockSpec((1, C, D_MODEL), lambda b: (b, 0, 0)),
                  pl.BlockSpec((1, 1, 6 * D_MODEL), lambda b: (ctx_row, 0, 0)),
                  _resident((1, D_MODEL)), _resident((1, KV_LORA)),
                  _resident(wc.shape), _resident(wkn.shape), _resident(wvt.shape)],
        out_specs=[pl.BlockSpec((1, C, N_HEADS * HEAD_PAD), lambda b: (b, 0, 0)),
                   pl.BlockSpec((1, N_HEADS * V_DIM, C), lambda b: (b, 0, 0))],
        out_shape=[jax.ShapeDtypeStruct((B, C, N_HEADS * HEAD_PAD), BF16),
                   jax.ShapeDtypeStruct((B, N_HEADS * V_DIM, C), BF16)],
        compiler_params=_params(("arbitrary",)),
        name="ctxproj",
    )(ctx, mod3, g_mix, g_kv, wc, wkn, wvt)


def _attn_kernel(qt_ref, kx_ref, kc_ref, vxt_ref, vct_ref, o_ref, s_ref):
    nq = qt_ref.shape[1]
    n_ctx = kc_ref.shape[1] // K_BLK
    nch = n_ctx + kx_ref.shape[1] // K_BLK

    def k_chunk(h, c):
        cols = slice(h * HEAD_PAD, (h + 1) * HEAD_PAD)
        if c < n_ctx:
            return kc_ref[0, c * K_BLK:(c + 1) * K_BLK, cols]
        return kx_ref[0, (c - n_ctx) * K_BLK:(c - n_ctx + 1) * K_BLK, cols]

    def vt_chunk(h, c):
        rows = slice(h * V_DIM, (h + 1) * V_DIM)
        if c < n_ctx:
            return vct_ref[0, rows, c * K_BLK:(c + 1) * K_BLK]
        return vxt_ref[0, rows, (c - n_ctx) * K_BLK:(c - n_ctx + 1) * K_BLK]

    def fold8(x, op):
        return op(x.reshape(K_BLK // 8, 8, Q_BLK), axis=0)

    items = [(h * nq + j, c) for h in range(ATTN_HEADS) for j in range(nq) for c in range(nch)]
    lead = nch + ATTN_LAG
    mxs = {}

    def scores(t):
        blk, c = items[t]
        h, j = divmod(blk, nq)
        qt = qt_ref[0, j, h * HEAD_PAD:(h + 1) * HEAD_PAD, :]
        s = _dot(k_chunk(h, c), qt)
        s_ref[blk % 2, c] = s
        r = fold8(s, jnp.max)
        mxs[blk] = r if blk not in mxs else jnp.maximum(mxs[blk], r)

    for t in range(min(lead, len(items))):
        scores(t)
    for g, (blk, c) in enumerate(items):
        h, j = divmod(blk, nq)
        if c == 0:
            m = jnp.max(mxs.pop(blk), axis=0, keepdims=True)
            l8, ot = None, None
        if g + lead < len(items):
            scores(g + lead)
        p = jnp.exp2(s_ref[blk % 2, c] - m)
        ps = fold8(p, jnp.sum)
        l8 = ps if l8 is None else l8 + ps
        d = _dot(vt_chunk(h, c), p.astype(BF16))
        ot = d if ot is None else ot + d
        if c == nch - 1:
            l = jnp.sum(l8, axis=0, keepdims=True)
            o_ref[0, j * Q_BLK:(j + 1) * Q_BLK, h * V_DIM:(h + 1) * V_DIM] = (
                (ot * (1.0 / l)).T.astype(o_ref.dtype))


def _attn_call(qt, kx, kc, vxt, vct):
    B, S, _ = kx.shape
    C = kc.shape[1]
    assert C % K_BLK == 0 and S % K_BLK == 0 and S % Q_BLK == 0
    nh = ATTN_HEADS
    assert N_HEADS % nh == 0
    return pl.pallas_call(
        _attn_kernel,
        grid=(B, N_HEADS // nh),
        in_specs=[pl.BlockSpec((1, S // Q_BLK, nh * HEAD_PAD, Q_BLK), lambda b, h: (b, 0, h, 0)),
                  pl.BlockSpec((1, S, nh * HEAD_PAD), lambda b, h: (b, 0, h)),
                  pl.BlockSpec((1, C, nh * HEAD_PAD), lambda b, h: (b, 0, h)),
                  pl.BlockSpec((1, nh * V_DIM, S), lambda b, h: (b, h, 0)),
                  pl.BlockSpec((1, nh * V_DIM, C), lambda b, h: (b, h, 0))],
        out_specs=pl.BlockSpec((1, S, nh * V_DIM), lambda b, h: (b, 0, h)),
        out_shape=jax.ShapeDtypeStruct((B, S, N_HEADS * V_DIM), BF16),
        scratch_shapes=[pltpu.VMEM((2, (C + S) // K_BLK, K_BLK, Q_BLK), F32)],
        compiler_params=_params(("arbitrary", "arbitrary")),
        name="attn",
    )(qt, kx, kc, vxt, vct)


def _tail_kernel(u_ref, att_ref, gate_ref, x_ref, mod_ref, w3_ref, b3_ref, lg_ref, lb_ref,
                 gffn_ref, gfin_ref, wpw_ref, wo_ref, wout_ref, w13_ref, w2_ref,
                 o_ref, ring_ref, y_ref, cv_ref, mg_ref, x1_ref, h2_ref, h_ref,
                 *, n_seq, n_tiles, cn):
    tm = x_ref.shape[1]
    d_ff = w2_ref.shape[0]
    sub = D_CONV // LANES
    g = pl.program_id(0)

    @pl.when(g == 0)
    def _():
        ring_ref[...] = jnp.zeros(ring_ref.shape, F32)
        cv_ref[...] = jnp.zeros(cv_ref.shape, BF16)

    def ingest():
        pos = lax.rem(g, n_seq)
        keep_prev = jnp.where(jnp.logical_or(pos == 0, g >= n_tiles), 0.0, 1.0).astype(F32)
        keep_next = jnp.where(pos == n_seq - 1, 0.0, 1.0).astype(F32)
        slot, nxt = lax.rem(g, CONV_RING), lax.rem(g + 1, CONV_RING)
        prv = lax.rem(g + CONV_RING - 1, CONV_RING)
        for s in range(sub):
            col = u_ref[0, :, s * LANES:(s + 1) * LANES]
            ring_ref[slot, pl.ds(CONV_HALO * sub + s, tm, stride=sub), :] = col
            ring_ref[prv, pl.ds((CONV_HALO + tm) * sub + s, CONV_HALO, stride=sub), :] = (
                col[0:CONV_HALO] * keep_prev)
            ring_ref[nxt, pl.ds(s, CONV_HALO, stride=sub), :] = col[tm - CONV_HALO:tm] * keep_next

    cslot = lax.rem(g + 2, CONV_RING)
    cpar = lax.rem(g, 2)
    bias, lg, lb = b3_ref[...], lg_ref[...], lb_ref[...]
    off = CONV_HALO - CONV_WIDTH // 2

    def conv_chunk(r, gate):
        acc = jnp.zeros((CONV_ROWS, sub, LANES), F32)
        for k in range(CONV_WIDTH):
            t0 = (r * CONV_ROWS + off + k) * sub
            win = ring_ref[cslot, t0:t0 + CONV_ROWS * sub, :]
            wk = w3_ref[k] if (k > 0 or gate is None) else w3_ref[k] + gate
            acc = acc + win.reshape(CONV_ROWS, sub, LANES) * wk
        y_ref[r] = (acc + bias).reshape(CONV_ROWS * sub, LANES)
        y = jnp.concatenate([y_ref[r, pl.ds(s, CONV_ROWS, stride=sub), :] for s in range(sub)], axis=-1)
        mu = jnp.mean(y, axis=-1, keepdims=True)
        yc = y - mu
        var = jnp.mean(yc * yc, axis=-1, keepdims=True)
        y = yc * lax.rsqrt(var + EPS) * lg + lb
        out = _silu(y)
        cv_ref[cpar, r * CONV_ROWS:(r + 1) * CONV_ROWS, :] = out.astype(BF16)
        return _zero_of(out[0:8, 0:LANES])

    m = mod_ref[0]
    gate_mix = m[:, 2 * D_MODEL:3 * D_MODEL]
    shift, scale = m[:, 3 * D_MODEL:4 * D_MODEL], m[:, 4 * D_MODEL:5 * D_MODEL]
    gate_ffn = m[:, 5 * D_MODEL:6 * D_MODEL]
    st = {}

    def anchor(ref, top, tok):
        if tok is not None:
            ref[0:16, 0:LANES] = (top + jnp.concatenate([tok, tok], axis=0)).astype(BF16)

    def merge_chunk(j, tok):
        del tok
        y_conv = _dot(cv_ref[1 - cpar], wpw_ref[:, j:j + cn])
        y_mla = _dot(att_ref[0], wo_ref[:, j:j + cn])
        merged = (gate_ref[0, :, j:j + cn].astype(F32) * y_conv
                  + gate_ref[0, :, D_MODEL + j:D_MODEL + j + cn].astype(F32) * y_mla)
        if j == 0:
            st["mg_top"] = merged[0:16, 0:LANES]
        mg_ref[:, j:j + cn] = merged.astype(BF16)
        return _zero_of(y_conv[0:8, 0:LANES])

    def out_chunk(j, tok):
        anchor(mg_ref, st["mg_top"], tok)
        yo = _dot(mg_ref[...], wout_ref[:, j:j + cn])
        x1_ref[:, j:j + cn] = x_ref[0, :, j:j + cn] + gate_mix[:, j:j + cn] * yo
        return _zero_of(yo[0:8, 0:LANES])

    def ffn_chunk(j, tok):
        if j == 0:
            h2 = _rms(x1_ref[...], gffn_ref[...]) * (1.0 + scale) + shift
            st["h2_top"] = h2[0:16, 0:LANES]
            h2_ref[...] = h2.astype(BF16)
        anchor(h2_ref, st["h2_top"], tok)
        ab = _dot(h2_ref[...], w13_ref[:, 2 * j:2 * j + 2 * LANES])
        hv = _silu(ab[:, 0:LANES]) * ab[:, LANES:]
        if j == 0:
            st["h_top"] = hv[0:16, :]
        h_ref[:, j:j + LANES] = hv.astype(BF16)
        return _zero_of(ab[0:8, 0:LANES])

    def down_chunk(j, tok):
        anchor(h_ref, st["h_top"], tok)
        yd = _dot(h_ref[...], w2_ref[:, j:j + cn])
        x1_ref[:, j:j + cn] = x1_ref[:, j:j + cn] + gate_ffn[:, j:j + cn] * yd
        return _zero_of(yd[0:8, 0:LANES])

    cols = range(0, D_MODEL, cn)
    mxu_tasks = ([functools.partial(merge_chunk, j) for j in cols]
                 + [functools.partial(out_chunk, j) for j in cols]
                 + [functools.partial(ffn_chunk, j) for j in range(0, d_ff, LANES)]
                 + [functools.partial(down_chunk, j) for j in cols])
    cost = ([0] + [2 * D_MODEL * cn] * (len(cols) - 1) + [D_MODEL * cn] * len(cols)
            + [2 * D_MODEL * LANES] * (d_ff // LANES) + [d_ff * cn] * len(cols))
    n_conv = tm // CONV_ROWS
    done = 0
    mxu_toks, group_toks = [None] * TAIL_LAG, [None] * TAIL_LAG
    for i, task in enumerate(mxu_tasks):
        want = (sum(cost[:i + 1]) * n_conv) // sum(cost)
        group = None
        for r in range(done, want):
            tok = conv_chunk(r, mxu_toks[-TAIL_LAG])
            group = tok if group is None else group + tok
        done = want
        mxu_toks.append(task(group_toks[-TAIL_LAG]))
        group_toks.append(group)
    o_ref[0] = _rms(x1_ref[...], gfin_ref[...])
    ingest()


def _tail_call(u, att, gates, x, mod3, w3, b3, ln_g, ln_b, g_ffn, g_final,
               w_pw, w_o, w_out, w13, w2, tm, cn):
    B, S, _ = x.shape
    n_seq = S // tm
    n_tiles = B * n_seq
    d_ff = w2.shape[0]
    sub = D_CONV // LANES
    assert tm % CONV_ROWS == 0 and d_ff % cn == 0

    def tile(t):
        return (t // n_seq, lax.rem(t, n_seq), 0)

    in_tile = lambda g: tile(jnp.minimum(g, n_tiles - 1))
    out_t = lambda g: jnp.clip(g - 3, 0, n_tiles - 1)
    out_tile = lambda g: tile(out_t(g))
    row = lambda w, f: pl.BlockSpec((1, tm, w), f)
    return pl.pallas_call(
        functools.partial(_tail_kernel, n_seq=n_seq, n_tiles=n_tiles, cn=cn),
        grid=(n_tiles + 3,),
        in_specs=[row(D_CONV, in_tile), row(N_HEADS * V_DIM, out_tile), row(2 * D_MODEL, out_tile),
                  row(D_MODEL, out_tile),
                  pl.BlockSpec((1, 1, 6 * D_MODEL), lambda g: (out_t(g) // n_seq, 0, 0)),
                  _resident(w3.shape), _resident(b3.shape), _resident((1, D_CONV)),
                  _resident((1, D_CONV)), _resident((1, D_MODEL)), _resident((1, D_MODEL)),
                  _resident(w_pw.shape), _resident(w_o.shape), _resident(w_out.shape),
                  _resident(w13.shape), _resident(w2.shape)],
        out_specs=row(D_MODEL, out_tile),
        out_shape=jax.ShapeDtypeStruct((B, S, D_MODEL), F32),
        scratch_shapes=[pltpu.VMEM((CONV_RING, (tm + 2 * CONV_HALO) * sub, LANES), F32),
                        pltpu.VMEM((tm // CONV_ROWS, CONV_ROWS * sub, LANES), F32),
                        pltpu.VMEM((2, tm, D_CONV), BF16),
                        pltpu.VMEM((tm, D_MODEL), BF16),
                        pltpu.VMEM((tm, D_MODEL), F32),
                        pltpu.VMEM((tm, D_MODEL), BF16),
                        pltpu.VMEM((tm, d_ff), BF16)],
        compiler_params=_params(("arbitrary",)),
        name="tail",
    )(u, att, gates, x, mod3, w3, b3, ln_g, ln_b, g_ffn, g_final, w_pw, w_o, w_out, w13, w2)


def _rope_tables(S):
    f32 = np.float32
    rows = S // GRID_W
    row = np.repeat(np.arange(rows, dtype=f32), GRID_W)
    col = np.tile(np.arange(GRID_W, dtype=f32), rows)
    axis_dim = QK_ROPE // 2
    inv_freq = (f32(ROPE_THETA) ** (-np.arange(0, axis_dim, 2, dtype=f32) / f32(axis_dim))).astype(f32)
    ar, ac = row[:, None] * inv_freq, col[:, None] * inv_freq
    cos = np.concatenate([np.cos(ar), np.cos(ar), np.cos(ac), np.cos(ac)], axis=-1).astype(f32)
    sin = np.concatenate([-np.sin(ar), np.sin(ar), -np.sin(ac), np.sin(ac)], axis=-1).astype(f32)
    z = np.zeros((S, LANES - QK_ROPE), f32)
    tabk = np.concatenate([cos, z, sin, z], axis=-1)
    tabq = np.ascontiguousarray((f32(Q_SCALE) * tabk).T)
    return jnp.asarray(tabq), jnp.asarray(tabk)


_ROPE_SWAP = np.concatenate([np.arange(16, 32), np.arange(0, 16),
                             np.arange(48, 64), np.arange(32, 48)])


def _prep_weights(w_in, w_uq, w_ukv):
    off_q, off_kv = 2 * D_CONV, 2 * D_CONV + Q_LORA
    off_kr, off_gate = off_kv + KV_LORA, off_kv + KV_LORA + QK_ROPE
    w_kr = w_in[:, off_kr:off_gate]
    z64 = jnp.zeros((D_MODEL, LANES - QK_ROPE), w_in.dtype)
    wglu = w_in[:, 0:off_q].astype(BF16)
    wgate = w_in[:, off_gate:].astype(BF16)
    wlat = jnp.concatenate([w_in[:, off_q:off_kr], w_kr, z64, w_kr[:, _ROPE_SWAP], z64],
                           axis=1).astype(BF16)
    wc = jnp.concatenate([w_in[:, off_kv:off_gate], z64], axis=1).astype(BF16)

    uq = w_uq.reshape(Q_LORA, N_HEADS, QK_NOPE + QK_ROPE)
    nope, rope = uq[:, :, :QK_NOPE], uq[:, :, QK_NOPE:]
    zq = jnp.zeros((Q_LORA, N_HEADS, LANES - QK_ROPE), w_uq.dtype)
    wq1t = jnp.concatenate([nope, rope, zq], axis=2).reshape(Q_LORA, N_HEADS * HEAD_PAD).T.astype(BF16)

    ukv = w_ukv.reshape(KV_LORA, N_HEADS, QK_NOPE + V_DIM)
    wkn = ukv[:, :, :QK_NOPE].reshape(KV_LORA, N_HEADS * QK_NOPE).astype(BF16)
    wvt = ukv[:, :, QK_NOPE:].reshape(KV_LORA, N_HEADS * V_DIM).T.astype(BF16)
    return wglu, wgate, wlat, wc, wq1t, wkn, wvt


def _interleave_w13(w13):
    d, two_ff = w13.shape
    nb = two_ff // (2 * LANES)
    return w13.reshape(d, 2, nb, LANES).transpose(0, 2, 1, 3).reshape(d, two_ff).astype(BF16)


def kernel(x, c, ctx, c_ctx, w_mod, b_mod, g_mix, g_ffn, w_in, g_q, g_kv, w_uq, w_ukv,
           w_o_mla, w_dw, b_dw, ln_g, ln_b, w_pw, w_out, w_13, w_2, g_final):
    B, S, _ = x.shape
    assert w_mod.shape[0] == 1, "single-layer block"
    assert B + 1 <= MOD_ROWS

    wglu, wgate, wlat, wc, wq1t, wkn, wvt = _prep_weights(w_in[0], w_uq[0], w_ukv[0])
    tabq, tabk = _rope_tables(S)
    w3 = w_dw[0].reshape(CONV_WIDTH, D_CONV // LANES, LANES)
    b3 = b_dw.reshape(D_CONV // LANES, LANES)

    cc = jnp.zeros((MOD_ROWS, D_MODEL), F32).at[0:B].set(c).at[B].set(c_ctx)
    mod3 = _mod_call(cc, w_mod[0], b_mod).reshape(MOD_ROWS, 1, 6 * D_MODEL)

    u, gates, qt, kx, vxt = _inproj_call(x, mod3, tabq, tabk, g_mix, g_q, g_kv,
                                         wglu, wgate, wlat, wq1t, wkn, wvt, tm=256)
    kc, vct = _ctxproj_call(ctx, mod3, g_mix, g_kv, wc, wkn, wvt, ctx_row=B)
    att = _attn_call(qt, kx, kc, vxt, vct)
    return _tail_call(u, att, gates, x, mod3, w3, b3, ln_g, ln_b, g_ffn, g_final.reshape(1, D_MODEL),
                      w_pw[0].astype(BF16), w_o_mla[0].astype(BF16), w_out[0].astype(BF16),
                      _interleave_w13(w_13[0]), w_2[0].astype(BF16), tm=256, cn=256)
```

```python
import functools

import jax
import jax.numpy as jnp
import numpy as np
from jax import lax
from jax.experimental import pallas as pl
from jax.experimental.pallas import tpu as pltpu

F32 = jnp.float32
BF16 = jnp.bfloat16

D_MODEL = 1024
N_HEADS = 8
QK_NOPE = 128
QK_ROPE = 64
V_DIM = 128
Q_LORA = 384
KV_LORA = 256
D_CONV = 1024
CONV_WIDTH = 31
CONV_HALO = 16
GRID_W = 64
ROPE_THETA = 10000.0
EPS = 1e-6
ATTN_SCALE = (QK_NOPE + QK_ROPE) ** -0.5
Q_SCALE = ATTN_SCALE * float(np.log2(np.e))
HEAD_PAD = 256
Q_BLK = 256
K_BLK = 256
ATTN_LAG = 6
ATTN_HEADS = 2
CONV_ROWS = 8
CONV_RING = 4
TAIL_LAG = 6
LANES = 128
MOD_ROWS = 24

VMEM_LIMIT = 56 * 1024 * 1024

_KR0 = Q_LORA + KV_LORA
_KRS0 = _KR0 + 128

_NT = (((1,), (1,)), ((), ()))


def _resident(shape):
    nd = len(shape)
    return pl.BlockSpec(shape, lambda *_: (0,) * nd, pipeline_mode=pl.Buffered(1))


def _params(sem):
    return pltpu.CompilerParams(dimension_semantics=sem, vmem_limit_bytes=VMEM_LIMIT)


def _rms(x, g):
    return x * lax.rsqrt(jnp.mean(x * x, axis=-1, keepdims=True) + EPS) * g


def _silu(x):
    return x * jax.nn.sigmoid(x)


def _dot(a, b):
    return jnp.dot(a, b, preferred_element_type=F32)


def _dot_nt(a, b):
    return lax.dot_general(a, b, _NT, preferred_element_type=F32)


def _zero_of(x):
    u = pltpu.bitcast(x, jnp.uint32)
    u = lax.shift_right_logical(lax.shift_right_logical(u, jnp.uint32(16)), jnp.uint32(16))
    return pltpu.bitcast(u, F32)


def _mod_kernel(cc_ref, w_ref, b_ref, o_ref):
    s = _silu(cc_ref[...])
    o_ref[...] = jnp.dot(s, w_ref[...], preferred_element_type=F32,
                         precision=lax.Precision.HIGHEST) + b_ref[...]


def _mod_call(cc, w_mod, b_mod):
    n = w_mod.shape[1]
    tn = 1024
    return pl.pallas_call(
        _mod_kernel,
        grid=(n // tn,),
        in_specs=[pl.BlockSpec((MOD_ROWS, D_MODEL), lambda j: (0, 0)),
                  pl.BlockSpec((D_MODEL, tn), lambda j: (0, j)),
                  pl.BlockSpec((1, tn), lambda j: (0, j))],
        out_specs=pl.BlockSpec((MOD_ROWS, tn), lambda j: (0, j)),
        out_shape=jax.ShapeDtypeStruct((MOD_ROWS, n), F32),
        compiler_params=_params(("arbitrary",)),
        name="mod",
    )(cc, w_mod, b_mod)


def _store_k(k_ref, kn, kr):
    krb = kr.astype(BF16)
    for h in range(N_HEADS):
        k_ref[0, :, h * HEAD_PAD:h * HEAD_PAD + QK_NOPE] = (
            kn[:, h * QK_NOPE:(h + 1) * QK_NOPE].astype(BF16))
        k_ref[0, :, h * HEAD_PAD + QK_NOPE:(h + 1) * HEAD_PAD] = krb


def _inproj_kernel(x_ref, mod_ref, tabq_ref, tabk_ref, gmix_ref, gq_ref, gkv_ref,
                   wglu_ref, wgate_ref, wlat_ref, wq1t_ref, wkn_ref, wvt_ref,
                   u_ref, gate_ref, qt_ref, k_ref, vt_ref):
    m = mod_ref[0]
    shift, scale = m[:, 0:D_MODEL], m[:, D_MODEL:2 * D_MODEL]
    hx = (_rms(x_ref[0], gmix_ref[...]) * (1.0 + scale) + shift).astype(BF16)

    cn = 512
    for j in range(0, D_CONV, cn):
        a = _dot(hx, wglu_ref[:, j:j + cn])
        g = _dot(hx, wglu_ref[:, D_CONV + j:D_CONV + j + cn])
        u_ref[0, :, j:j + cn] = a * jax.nn.sigmoid(g)
    for j in range(0, 2 * D_MODEL, cn):
        gate_ref[0, :, j:j + cn] = jax.nn.sigmoid(_dot(hx, wgate_ref[:, j:j + cn])).astype(BF16)

    z = _dot(hx, wlat_ref[...])
    qa = z[:, 0:Q_LORA]
    kva = z[:, Q_LORA:Q_LORA + KV_LORA]
    kr_a = z[:, _KR0:_KR0 + LANES]
    kr_b = z[:, _KRS0:_KRS0 + LANES]

    tabq = tabq_ref[...]
    tq_c, tq_s = tabq[0:LANES], tabq[LANES:2 * LANES]
    tabk = tabk_ref[...]
    tk_c, tk_s = tabk[:, 0:LANES], tabk[:, LANES:2 * LANES]

    qn = _rms(qa, gq_ref[...]).astype(BF16)
    qf = _dot_nt(wq1t_ref[...], qn)
    half = QK_ROPE // 4
    for h in range(N_HEADS):
        c0 = h * HEAD_PAD
        q_nope = (qf[c0:c0 + QK_NOPE] * Q_SCALE).astype(BF16)
        qr = qf[c0 + QK_NOPE:c0 + HEAD_PAD]
        qr_swap = jnp.concatenate([qr[half:2 * half], qr[0:half], qr[3 * half:4 * half],
                                   qr[2 * half:3 * half], qr[4 * half:]], axis=0)
        q_rope = (qr * tq_c + qr_swap * tq_s).astype(BF16)
        for jb in range(qt_ref.shape[1]):
            qt_ref[0, jb, c0:c0 + QK_NOPE, :] = q_nope[:, jb * Q_BLK:(jb + 1) * Q_BLK]
            qt_ref[0, jb, c0 + QK_NOPE:c0 + HEAD_PAD, :] = q_rope[:, jb * Q_BLK:(jb + 1) * Q_BLK]

    kvn = _rms(kva, gkv_ref[...]).astype(BF16)
    _store_k(k_ref, _dot(kvn, wkn_ref[...]), kr_a * tk_c + kr_b * tk_s)
    vt_ref[0] = _dot_nt(wvt_ref[...], kvn).astype(BF16)


def _inproj_call(x, mod3, tabq, tabk, g_mix, g_q, g_kv, wglu, wgate, wlat, wq1t, wkn, wvt, tm):
    B, S, _ = x.shape
    row = lambda w: pl.BlockSpec((1, tm, w), lambda b, i: (b, i, 0))
    col = lambda w: pl.BlockSpec((1, w, tm), lambda b, i: (b, 0, i))
    return pl.pallas_call(
        _inproj_kernel,
        grid=(B, S // tm),
        in_specs=[row(D_MODEL),
                  pl.BlockSpec((1, 1, 6 * D_MODEL), lambda b, i: (b, 0, 0)),
                  pl.BlockSpec((2 * LANES, tm), lambda b, i: (0, i)),
                  pl.BlockSpec((tm, 2 * LANES), lambda b, i: (i, 0)),
                  _resident((1, D_MODEL)), _resident((1, Q_LORA)), _resident((1, KV_LORA)),
                  _resident(wglu.shape), _resident(wgate.shape), _resident(wlat.shape),
                  _resident(wq1t.shape),
                  _resident(wkn.shape), _resident(wvt.shape)],
        out_specs=[row(D_CONV), row(2 * D_MODEL),
                   pl.BlockSpec((1, tm // Q_BLK, N_HEADS * HEAD_PAD, Q_BLK), lambda b, i: (b, i, 0, 0)),
                   row(N_HEADS * HEAD_PAD), col(N_HEADS * V_DIM)],
        out_shape=[jax.ShapeDtypeStruct((B, S, D_CONV), F32),
                   jax.ShapeDtypeStruct((B, S, 2 * D_MODEL), BF16),
                   jax.ShapeDtypeStruct((B, S // Q_BLK, N_HEADS * HEAD_PAD, Q_BLK), BF16),
                   jax.ShapeDtypeStruct((B, S, N_HEADS * HEAD_PAD), BF16),
                   jax.ShapeDtypeStruct((B, N_HEADS * V_DIM, S), BF16)],
        compiler_params=_params(("arbitrary", "arbitrary")),
        name="inproj",
    )(x, mod3, tabq, tabk, g_mix, g_q, g_kv, wglu, wgate, wlat, wq1t, wkn, wvt)


def _ctxproj_kernel(c_ref, mod_ref, gmix_ref, gkv_ref, wc_ref, wkn_ref, wvt_ref, k_ref, vt_ref):
    m = mod_ref[0]
    shift, scale = m[:, 0:D_MODEL], m[:, D_MODEL:2 * D_MODEL]
    hc = (_rms(c_ref[0], gmix_ref[...]) * (1.0 + scale) + shift).astype(BF16)
    z = _dot(hc, wc_ref[...])
    kvn = _rms(z[:, 0:KV_LORA], gkv_ref[...]).astype(BF16)
    _store_k(k_ref, _dot(kvn, wkn_ref[...]), z[:, KV_LORA:KV_LORA + LANES])
    vt_ref[0] = _dot_nt(wvt_ref[...], kvn).astype(BF16)


def _ctxproj_call(ctx, mod3, g_mix, g_kv, wc, wkn, wvt, ctx_row):
    B, C, _ = ctx.shape
    return pl.pallas_call(
        _ctxproj_kernel,
        grid=(B,),
        in_specs=[pl.BlockSpec((1, C, D_MODEL), lambda b: (b, 0, 0)),
                  pl.BlockSpec((1, 1, 6 * D_MODEL), lambda b: (ctx_row, 0, 0)),
                  _resident((1, D_MODEL)), _resident((1, KV_LORA)),
                  _resident(wc.shape), _resident(wkn.shape), _resident(wvt.shape)],
        out_specs=[pl.BlockSpec((1, C, N_HEADS * HEAD_PAD), lambda b: (b, 0, 0)),
                   pl.BlockSpec((1, N_HEADS * V_DIM, C), lambda b: (b, 0, 0))],
        out_shape=[jax.ShapeDtypeStruct((B, C, N_HEADS * HEAD_PAD), BF16),
                   jax.ShapeDtypeStruct((B, N_HEADS * V_DIM, C), BF16)],
        compiler_params=_params(("arbitrary",)),
        name="ctxproj",
    )(ctx, mod3, g_mix, g_kv, wc, wkn, wvt)


def _attn_kernel(qt_ref, kx_ref, kc_ref, vxt_ref, vct_ref, o_ref, s_ref):
    nq = qt_ref.shape[1]
    n_ctx = kc_ref.shape[1] // K_BLK
    nch = n_ctx + kx_ref.shape[1] // K_BLK

    def k_chunk(h, c):
        cols = slice(h * HEAD_PAD, (h + 1) * HEAD_PAD)
        if c < n_ctx:
            return kc_ref[0, c * K_BLK:(c + 1) * K_BLK, cols]
        return kx_ref[0, (c - n_ctx) * K_BLK:(c - n_ctx + 1) * K_BLK, cols]

    def vt_chunk(h, c):
        rows = slice(h * V_DIM, (h + 1) * V_DIM)
        if c < n_ctx:
            return vct_ref[0, rows, c * K_BLK:(c + 1) * K_BLK]
        return vxt_ref[0, rows, (c - n_ctx) * K_BLK:(c - n_ctx + 1) * K_BLK]

    def fold8(x, op):
        return op(x.reshape(K_BLK // 8, 8, Q_BLK), axis=0)

    items = [(h * nq + j, c) for h in range(ATTN_HEADS) for j in range(nq) for c in range(nch)]
    lead = nch + ATTN_LAG
    mxs = {}

    def scores(t):
        blk, c = items[t]
        h, j = divmod(blk, nq)
        qt = qt_ref[0, j, h * HEAD_PAD:(h + 1) * HEAD_PAD, :]
        s = _dot(k_chunk(h, c), qt)
        s_ref[blk % 2, c] = s
        r = fold8(s, jnp.max)
        mxs[blk] = r if blk not in mxs else jnp.maximum(mxs[blk], r)

    for t in range(min(lead, len(items))):
        scores(t)
    for g, (blk, c) in enumerate(items):
        h, j = divmod(blk, nq)
        if c == 0:
            m = jnp.max(mxs.pop(blk), axis=0, keepdims=True)
            l8, ot = None, None
        if g + lead < len(items):
            scores(g + lead)
        p = jnp.exp2(s_ref[blk % 2, c] - m)
        ps = fold8(p, jnp.sum)
        l8 = ps if l8 is None else l8 + ps
        d = _dot(vt_chunk(h, c), p.astype(BF16))
        ot = d if ot is None else ot + d
        if c == nch - 1:
            l = jnp.sum(l8, axis=0, keepdims=True)
            o_ref[0, j * Q_BLK:(j + 1) * Q_BLK, h * V_DIM:(h + 1) * V_DIM] = (
                (ot * (1.0 / l)).T.astype(o_ref.dtype))


def _attn_call(qt, kx, kc, vxt, vct):
    B, S, _ = kx.shape
    C = kc.shape[1]
    assert C % K_BLK == 0 and S % K_BLK == 0 and S % Q_BLK == 0
    nh = ATTN_HEADS
    assert N_HEADS % nh == 0
    return pl.pallas_call(
        _attn_kernel,
        grid=(B, N_HEADS // nh),
        in_specs=[pl.BlockSpec((1, S // Q_BLK, nh * HEAD_PAD, Q_BLK), lambda b, h: (b, 0, h, 0)),
                  pl.BlockSpec((1, S, nh * HEAD_PAD), lambda b, h: (b, 0, h)),
                  pl.BlockSpec((1, C, nh * HEAD_PAD), lambda b, h: (b, 0, h)),
                  pl.BlockSpec((1, nh * V_DIM, S), lambda b, h: (b, h, 0)),
                  pl.BlockSpec((1, nh * V_DIM, C), lambda b, h: (b, h, 0))],
        out_specs=pl.BlockSpec((1, S, nh * V_DIM), lambda b, h: (b, 0, h)),
        out_shape=jax.ShapeDtypeStruct((B, S, N_HEADS * V_DIM), BF16),
        scratch_shapes=[pltpu.VMEM((2, (C + S) // K_BLK, K_BLK, Q_BLK), F32)],
        compiler_params=_params(("arbitrary", "arbitrary")),
        name="attn",
    )(qt, kx, kc, vxt, vct)


def _tail_kernel(u_ref, att_ref, gate_ref, x_ref, mod_ref, w3_ref, b3_ref, lg_ref, lb_ref,
                 gffn_ref, gfin_ref, wpw_ref, wo_ref, wout_ref, w13_ref, w2_ref,
                 o_ref, ring_ref, pk_ref, y_ref, cv_ref, mg_ref, x1_ref, h2_ref, h_ref,
                 *, n_seq, n_tiles, cn):
    tm = x_ref.shape[1]
    d_ff = w2_ref.shape[0]
    sub = D_CONV // LANES
    g = pl.program_id(0)

    @pl.when(g == 0)
    def _():
        ring_ref[...] = jnp.zeros(ring_ref.shape, F32)
        cv_ref[...] = jnp.zeros(cv_ref.shape, BF16)

    def ingest():
        pos = lax.rem(g, n_seq)
        keep_prev = jnp.where(jnp.logical_or(pos == 0, g >= n_tiles), 0.0, 1.0).astype(F32)
        keep_next = jnp.where(pos == n_seq - 1, 0.0, 1.0).astype(F32)
        slot, nxt = lax.rem(g, CONV_RING), lax.rem(g + 1, CONV_RING)
        prv = lax.rem(g + CONV_RING - 1, CONV_RING)
        for s in range(sub):
            col = u_ref[0, :, s * LANES:(s + 1) * LANES]
            ring_ref[slot, pl.ds(CONV_HALO * sub + s, tm, stride=sub), :] = col
            ring_ref[prv, pl.ds((CONV_HALO + tm) * sub + s, CONV_HALO, stride=sub), :] = (
                col[0:CONV_HALO] * keep_prev)
            ring_ref[nxt, pl.ds(s, CONV_HALO, stride=sub), :] = col[tm - CONV_HALO:tm] * keep_next

    cslot = lax.rem(g + 2, CONV_RING)
    cpar = lax.rem(g, 2)
    bias, lg, lb = b3_ref[...], lg_ref[...], lb_ref[...]
    off = CONV_HALO - CONV_WIDTH // 2

    half = tm // 2
    for jt in range(half + 2 * CONV_HALO):
        lo = ring_ref[cslot, jt * sub:(jt + 1) * sub, :]
        hi = ring_ref[cslot, (jt + half) * sub:(jt + half + 1) * sub, :]
        pk_ref[jt * 2 * sub:(jt + 1) * 2 * sub, :] = jnp.concatenate([lo, hi], axis=0).astype(BF16)

    def conv_chunk(r, gate):
        acc = jnp.zeros((CONV_ROWS, 2 * sub, LANES), BF16)
        for k in range(CONV_WIDTH):
            t0 = (r * CONV_ROWS + off + k) * 2 * sub
            win = pk_ref[t0:t0 + CONV_ROWS * 2 * sub, :]
            wk = w3_ref[k]
            if k == 0 and gate is not None:
                wk = wk + jnp.concatenate([gate, gate], axis=0).astype(BF16)
            acc = acc + win.reshape(CONV_ROWS, 2 * sub, LANES) * wk
        accf = acc.astype(F32)
        for hb in range(2):
            yh = accf[:, hb * sub:(hb + 1) * sub, :] + bias
            y_ref[2 * r + hb] = yh.reshape(CONV_ROWS * sub, LANES)
            y = jnp.concatenate([y_ref[2 * r + hb, pl.ds(s, CONV_ROWS, stride=sub), :]
                                 for s in range(sub)], axis=-1)
            mu = jnp.mean(y, axis=-1, keepdims=True)
            yc = y - mu
            var = jnp.mean(yc * yc, axis=-1, keepdims=True)
            y = yc * lax.rsqrt(var + EPS) * lg + lb
            out = _silu(y)
            row0 = hb * half + r * CONV_ROWS
            cv_ref[cpar, row0:row0 + CONV_ROWS, :] = out.astype(BF16)
        return _zero_of(out[0:8, 0:LANES])

    m = mod_ref[0]
    gate_mix = m[:, 2 * D_MODEL:3 * D_MODEL]
    shift, scale = m[:, 3 * D_MODEL:4 * D_MODEL], m[:, 4 * D_MODEL:5 * D_MODEL]
    gate_ffn = m[:, 5 * D_MODEL:6 * D_MODEL]
    st = {}

    def anchor(ref, top, tok):
        if tok is not None:
            ref[0:16, 0:LANES] = (top + jnp.concatenate([tok, tok], axis=0)).astype(BF16)

    def merge_chunk(j, tok):
        del tok
        y_conv = _dot(cv_ref[1 - cpar], wpw_ref[:, j:j + cn])
        y_mla = _dot(att_ref[0], wo_ref[:, j:j + cn])
        merged = (gate_ref[0, :, j:j + cn].astype(F32) * y_conv
                  + gate_ref[0, :, D_MODEL + j:D_MODEL + j + cn].astype(F32) * y_mla)
        if j == 0:
            st["mg_top"] = merged[0:16, 0:LANES]
        mg_ref[:, j:j + cn] = merged.astype(BF16)
        return _zero_of(y_conv[0:8, 0:LANES])

    def out_chunk(j, tok):
        anchor(mg_ref, st["mg_top"], tok)
        yo = _dot(mg_ref[...], wout_ref[:, j:j + cn])
        x1_ref[:, j:j + cn] = x_ref[0, :, j:j + cn] + gate_mix[:, j:j + cn] * yo
        return _zero_of(yo[0:8, 0:LANES])

    def ffn_chunk(j, tok):
        if j == 0:
            h2 = _rms(x1_ref[...], gffn_ref[...]) * (1.0 + scale) + shift
            st["h2_top"] = h2[0:16, 0:LANES]
            h2_ref[...] = h2.astype(BF16)
        anchor(h2_ref, st["h2_top"], tok)
        w_ab = jnp.concatenate([w13_ref[:, j:j + LANES], w13_ref[:, d_ff + j:d_ff + j + LANES]], axis=1)
        ab = _dot(h2_ref[...], w_ab)
        hv = _silu(ab[:, 0:LANES]) * ab[:, LANES:]
        if j == 0:
            st["h_top"] = hv[0:16, :]
        h_ref[:, j:j + LANES] = hv.astype(BF16)
        return _zero_of(ab[0:8, 0:LANES])

    def down_chunk(j, tok):
        anchor(h_ref, st["h_top"], tok)
        yd = _dot(h_ref[...], w2_ref[:, j:j + cn])
        x1_ref[:, j:j + cn] = x1_ref[:, j:j + cn] + gate_ffn[:, j:j + cn] * yd
        return _zero_of(yd[0:8, 0:LANES])

    cols = range(0, D_MODEL, cn)
    mxu_tasks = ([functools.partial(merge_chunk, j) for j in cols]
                 + [functools.partial(out_chunk, j) for j in cols]
                 + [functools.partial(ffn_chunk, j) for j in range(0, d_ff, LANES)]
                 + [functools.partial(down_chunk, j) for j in cols])
    cost = ([0] + [2 * D_MODEL * cn] * (len(cols) - 1) + [D_MODEL * cn] * len(cols)
            + [2 * D_MODEL * LANES] * (d_ff // LANES) + [d_ff * cn] * len(cols))
    n_conv = half // CONV_ROWS
    done = 0
    mxu_toks, group_toks = [None] * TAIL_LAG, [None] * TAIL_LAG
    for i, task in enumerate(mxu_tasks):
        want = (sum(cost[:i + 1]) * n_conv) // sum(cost)
        group = None
        for r in range(done, want):
            tok = conv_chunk(r, mxu_toks[-TAIL_LAG])
            group = tok if group is None else group + tok
        done = want
        mxu_toks.append(task(group_toks[-TAIL_LAG]))
        group_toks.append(group)
    o_ref[0] = _rms(x1_ref[...], gfin_ref[...])
    ingest()


def _tail_call(u, att, gates, x, mod3, w3, b3, ln_g, ln_b, g_ffn, g_final,
               w_pw, w_o, w_out, w13, w2, tm, cn):
    B, S, _ = x.shape
    n_seq = S // tm
    n_tiles = B * n_seq
    d_ff = w2.shape[0]
    sub = D_CONV // LANES
    assert tm % CONV_ROWS == 0 and d_ff % cn == 0

    def tile(t):
        return (t // n_seq, lax.rem(t, n_seq), 0)

    in_tile = lambda g: tile(jnp.minimum(g, n_tiles - 1))
    out_t = lambda g: jnp.clip(g - 3, 0, n_tiles - 1)
    out_tile = lambda g: tile(out_t(g))
    row = lambda w, f: pl.BlockSpec((1, tm, w), f)
    return pl.pallas_call(
        functools.partial(_tail_kernel, n_seq=n_seq, n_tiles=n_tiles, cn=cn),
        grid=(n_tiles + 3,),
        in_specs=[row(D_CONV, in_tile), row(N_HEADS * V_DIM, out_tile), row(2 * D_MODEL, out_tile),
                  row(D_MODEL, out_tile),
                  pl.BlockSpec((1, 1, 6 * D_MODEL), lambda g: (out_t(g) // n_seq, 0, 0)),
                  _resident(w3.shape), _resident(b3.shape), _resident((1, D_CONV)),
                  _resident((1, D_CONV)), _resident((1, D_MODEL)), _resident((1, D_MODEL)),
                  _resident(w_pw.shape), _resident(w_o.shape), _resident(w_out.shape),
                  _resident(w13.shape), _resident(w2.shape)],
        out_specs=row(D_MODEL, out_tile),
        out_shape=jax.ShapeDtypeStruct((B, S, D_MODEL), F32),
        scratch_shapes=[pltpu.VMEM((CONV_RING, (tm + 2 * CONV_HALO) * sub, LANES), F32),
                        pltpu.VMEM(((tm // 2 + 2 * CONV_HALO) * 2 * sub, LANES), BF16),
                        pltpu.VMEM((tm // CONV_ROWS, CONV_ROWS * sub, LANES), F32),
                        pltpu.VMEM((2, tm, D_CONV), BF16),
                        pltpu.VMEM((tm, D_MODEL), BF16),
                        pltpu.VMEM((tm, D_MODEL), F32),
                        pltpu.VMEM((tm, D_MODEL), BF16),
                        pltpu.VMEM((tm, d_ff), BF16)],
        compiler_params=_params(("arbitrary",)),
        name="tail",
    )(u, att, gates, x, mod3, w3, b3, ln_g, ln_b, g_ffn, g_final, w_pw, w_o, w_out, w13, w2)


def _rope_tables(S):
    f32 = np.float32
    rows = S // GRID_W
    row = np.repeat(np.arange(rows, dtype=f32), GRID_W)
    col = np.tile(np.arange(GRID_W, dtype=f32), rows)
    axis_dim = QK_ROPE // 2
    inv_freq = (f32(ROPE_THETA) ** (-np.arange(0, axis_dim, 2, dtype=f32) / f32(axis_dim))).astype(f32)
    ar, ac = row[:, None] * inv_freq, col[:, None] * inv_freq
    cos = np.concatenate([np.cos(ar), np.cos(ar), np.cos(ac), np.cos(ac)], axis=-1).astype(f32)
    sin = np.concatenate([-np.sin(ar), np.sin(ar), -np.sin(ac), np.sin(ac)], axis=-1).astype(f32)
    z = np.zeros((S, LANES - QK_ROPE), f32)
    tabk = np.concatenate([cos, z, sin, z], axis=-1)
    tabq = np.ascontiguousarray((f32(Q_SCALE) * tabk).T)
    return jnp.asarray(tabq), jnp.asarray(tabk)


_ROPE_SWAP = np.concatenate([np.arange(16, 32), np.arange(0, 16),
                             np.arange(48, 64), np.arange(32, 48)])


def _prep_weights(w_in, w_uq, w_ukv):
    off_q, off_kv = 2 * D_CONV, 2 * D_CONV + Q_LORA
    off_kr, off_gate = off_kv + KV_LORA, off_kv + KV_LORA + QK_ROPE
    w_kr = w_in[:, off_kr:off_gate]
    z64 = jnp.zeros((D_MODEL, LANES - QK_ROPE), w_in.dtype)
    wglu = w_in[:, 0:off_q].astype(BF16)
    wgate = w_in[:, off_gate:].astype(BF16)
    wlat = jnp.concatenate([w_in[:, off_q:off_kr], w_kr, z64, w_kr[:, _ROPE_SWAP], z64],
                           axis=1).astype(BF16)
    wc = jnp.concatenate([w_in[:, off_kv:off_gate], z64], axis=1).astype(BF16)

    uq = w_uq.reshape(Q_LORA, N_HEADS, QK_NOPE + QK_ROPE)
    nope, rope = uq[:, :, :QK_NOPE], uq[:, :, QK_NOPE:]
    zq = jnp.zeros((Q_LORA, N_HEADS, LANES - QK_ROPE), w_uq.dtype)
    wq1t = jnp.concatenate([nope, rope, zq], axis=2).reshape(Q_LORA, N_HEADS * HEAD_PAD).T.astype(BF16)

    ukv = w_ukv.reshape(KV_LORA, N_HEADS, QK_NOPE + V_DIM)
    wkn = ukv[:, :, :QK_NOPE].reshape(KV_LORA, N_HEADS * QK_NOPE).astype(BF16)
    wvt = ukv[:, :, QK_NOPE:].reshape(KV_LORA, N_HEADS * V_DIM).T.astype(BF16)
    return wglu, wgate, wlat, wc, wq1t, wkn, wvt


def kernel(x, c, ctx, c_ctx, w_mod, b_mod, g_mix, g_ffn, w_in, g_q, g_kv, w_uq, w_ukv,
           w_o_mla, w_dw, b_dw, ln_g, ln_b, w_pw, w_out, w_13, w_2, g_final):
    B, S, _ = x.shape
    assert w_mod.shape[0] == 1, "single-layer block"
    assert B + 1 <= MOD_ROWS

    wglu, wgate, wlat, wc, wq1t, wkn, wvt = _prep_weights(w_in[0], w_uq[0], w_ukv[0])
    tabq, tabk = _rope_tables(S)
    w3 = w_dw[0].reshape(CONV_WIDTH, D_CONV // LANES, LANES)
    w3 = jnp.concatenate([w3, w3], axis=1).astype(BF16)
    b3 = b_dw.reshape(D_CONV // LANES, LANES)

    cc = jnp.zeros((MOD_ROWS, D_MODEL), F32).at[0:B].set(c).at[B].set(c_ctx)
    mod3 = _mod_call(cc, w_mod[0], b_mod).reshape(MOD_ROWS, 1, 6 * D_MODEL)

    u, gates, qt, kx, vxt = _inproj_call(x, mod3, tabq, tabk, g_mix, g_q, g_kv,
                                         wglu, wgate, wlat, wq1t, wkn, wvt, tm=256)
    kc, vct = _ctxproj_call(ctx, mod3, g_mix, g_kv, wc, wkn, wvt, ctx_row=B)
    att = _attn_call(qt, kx, kc, vxt, vct)
    return _tail_call(u, att, gates, x, mod3, w3, b3, ln_g, ln_b, g_ffn, g_final.reshape(1, D_MODEL),
                      w_pw[0].astype(BF16), w_o_mla[0].astype(BF16), w_out[0].astype(BF16),
                      w_13[0].astype(BF16), w_2[0].astype(BF16), tm=256, cn=256)
```

```python
import functools

import jax
import jax.numpy as jnp
import numpy as np
from jax import lax
from jax.experimental import pallas as pl
from jax.experimental.pallas import tpu as pltpu

F32 = jnp.float32
BF16 = jnp.bfloat16

D_MODEL = 1024
N_HEADS = 8
QK_NOPE = 128
QK_ROPE = 64
V_DIM = 128
Q_LORA = 384
KV_LORA = 256
D_CONV = 1024
CONV_WIDTH = 31
CONV_HALO = 16
GRID_W = 64
ROPE_THETA = 10000.0
EPS = 1e-6
ATTN_SCALE = (QK_NOPE + QK_ROPE) ** -0.5
Q_SCALE = ATTN_SCALE * float(np.log2(np.e))
HEAD_PAD = 256
Q_BLK = 256
K_BLK = 256
ATTN_LAG = 8
ATTN_HEADS = 2
CONV_ROWS = 8
CONV_RING = 4
TAIL_LAG = 6
LANES = 128
MOD_ROWS = 24

VMEM_LIMIT = 56 * 1024 * 1024

_KR0 = Q_LORA + KV_LORA

_NT = (((1,), (1,)), ((), ()))


def _resident(shape):
    nd = len(shape)
    return pl.BlockSpec(shape, lambda *_: (0,) * nd, pipeline_mode=pl.Buffered(1))


def _params(sem):
    return pltpu.CompilerParams(dimension_semantics=sem, vmem_limit_bytes=VMEM_LIMIT)


def _rms(x, g):
    return x * lax.rsqrt(jnp.mean(x * x, axis=-1, keepdims=True) + EPS) * g


def _silu(x):
    return x * jax.nn.sigmoid(x)


def _dot(a, b):
    return jnp.dot(a, b, preferred_element_type=F32)


def _dot_nt(a, b):
    return lax.dot_general(a, b, _NT, preferred_element_type=F32)


def _zero_of(x):
    u = pltpu.bitcast(x, jnp.uint32)
    u = lax.shift_right_logical(lax.shift_right_logical(u, jnp.uint32(16)), jnp.uint32(16))
    return pltpu.bitcast(u, F32)


def _mod_kernel(cc_ref, w_ref, b_ref, o_ref):
    s = _silu(cc_ref[...])
    o_ref[...] = jnp.dot(s, w_ref[...], preferred_element_type=F32,
                         precision=lax.Precision.HIGHEST) + b_ref[...]


def _mod_call(cc, w_mod, b_mod):
    n = w_mod.shape[1]
    tn = 1024
    return pl.pallas_call(
        _mod_kernel,
        grid=(n // tn,),
        in_specs=[pl.BlockSpec((MOD_ROWS, D_MODEL), lambda j: (0, 0)),
                  pl.BlockSpec((D_MODEL, tn), lambda j: (0, j)),
                  pl.BlockSpec((1, tn), lambda j: (0, j))],
        out_specs=pl.BlockSpec((MOD_ROWS, tn), lambda j: (0, j)),
        out_shape=jax.ShapeDtypeStruct((MOD_ROWS, n), F32),
        compiler_params=_params(("arbitrary",)),
        name="mod",
    )(cc, w_mod, b_mod)


def _store_k(k_ref, kn, kr):
    krb = kr.astype(BF16)
    for h in range(N_HEADS):
        k_ref[0, :, h * HEAD_PAD:h * HEAD_PAD + QK_NOPE] = (
            kn[:, h * QK_NOPE:(h + 1) * QK_NOPE].astype(BF16))
        k_ref[0, :, h * HEAD_PAD + QK_NOPE:(h + 1) * HEAD_PAD] = krb


def _inproj_kernel(x_ref, mod_ref, tabq_ref, tabk_ref, gmix_ref, gq_ref, gkv_ref,
                   wglu_ref, wgate_ref, wlat_ref, wq1t_ref, wkn_ref, wvt_ref,
                   u_ref, gate_ref, qt_ref, k_ref, vt_ref):
    m = mod_ref[0]
    shift, scale = m[:, 0:D_MODEL], m[:, D_MODEL:2 * D_MODEL]
    hx = (_rms(x_ref[0], gmix_ref[...]) * (1.0 + scale) + shift).astype(BF16)

    cn = 512

    def glu(j):
        a = _dot(hx, wglu_ref[:, j:j + cn])
        g = _dot(hx, wglu_ref[:, D_CONV + j:D_CONV + j + cn])
        u_ref[0, :, j:j + cn] = a * jax.nn.sigmoid(g)

    def gates(j):
        gate_ref[0, :, j:j + cn] = jax.nn.sigmoid(_dot(hx, wgate_ref[:, j:j + cn])).astype(BF16)

    z = _dot(hx, wlat_ref[...])
    glu(0)
    qa = z[:, 0:Q_LORA]
    kva = z[:, Q_LORA:Q_LORA + KV_LORA]
    qn = _rms(qa, gq_ref[...]).astype(BF16)
    kvn = _rms(kva, gkv_ref[...]).astype(BF16)
    qf = _dot_nt(wq1t_ref[...], qn)
    kn = _dot(kvn, wkn_ref[...])
    vt = _dot_nt(wvt_ref[...], kvn)
    glu(cn)
    gates(0)

    kr_prod = z[:, _KR0:_KR0 + LANES] * tabk_ref[...]
    kr_sum = kr_prod + pltpu.roll(kr_prod, LANES // 2, axis=1)
    lane = lax.broadcasted_iota(jnp.int32, (1, LANES), 1)
    _store_k(k_ref, kn, jnp.where(lane < QK_ROPE, kr_sum, 0.0))
    vt_ref[0] = vt.astype(BF16)

    tabq = tabq_ref[...]
    tq_c, tq_s = tabq[0:LANES], tabq[LANES:2 * LANES]
    half = QK_ROPE // 4
    for h in range(N_HEADS):
        c0 = h * HEAD_PAD
        q_nope = (qf[c0:c0 + QK_NOPE] * Q_SCALE).astype(BF16)
        qr = qf[c0 + QK_NOPE:c0 + HEAD_PAD]
        qr_swap = jnp.concatenate([qr[half:2 * half], qr[0:half], qr[3 * half:4 * half],
                                   qr[2 * half:3 * half], qr[4 * half:]], axis=0)
        q_rope = (qr * tq_c + qr_swap * tq_s).astype(BF16)
        for jb in range(qt_ref.shape[1]):
            qt_ref[0, jb, c0:c0 + QK_NOPE, :] = q_nope[:, jb * Q_BLK:(jb + 1) * Q_BLK]
            qt_ref[0, jb, c0 + QK_NOPE:c0 + HEAD_PAD, :] = q_rope[:, jb * Q_BLK:(jb + 1) * Q_BLK]
        if h % 3 == 2:
            gates(cn * (1 + h // 3))
    gates(3 * cn)


def _inproj_call(x, mod3, tabq, tabk, g_mix, g_q, g_kv, wglu, wgate, wlat, wq1t, wkn, wvt, tm):
    B, S, _ = x.shape
    row = lambda w: pl.BlockSpec((1, tm, w), lambda b, i: (b, i, 0))
    col = lambda w: pl.BlockSpec((1, w, tm), lambda b, i: (b, 0, i))
    return pl.pallas_call(
        _inproj_kernel,
        grid=(B, S // tm),
        in_specs=[row(D_MODEL),
                  pl.BlockSpec((1, 1, 6 * D_MODEL), lambda b, i: (b, 0, 0)),
                  pl.BlockSpec((2 * LANES, tm), lambda b, i: (0, i)),
                  pl.BlockSpec((tm, LANES), lambda b, i: (i, 0)),
                  _resident((1, D_MODEL)), _resident((1, Q_LORA)), _resident((1, KV_LORA)),
                  _resident(wglu.shape), _resident(wgate.shape), _resident(wlat.shape),
                  _resident(wq1t.shape),
                  _resident(wkn.shape), _resident(wvt.shape)],
        out_specs=[row(D_CONV), row(2 * D_MODEL),
                   pl.BlockSpec((1, tm // Q_BLK, N_HEADS * HEAD_PAD, Q_BLK), lambda b, i: (b, i, 0, 0)),
                   row(N_HEADS * HEAD_PAD), col(N_HEADS * V_DIM)],
        out_shape=[jax.ShapeDtypeStruct((B, S, D_CONV), F32),
                   jax.ShapeDtypeStruct((B, S, 2 * D_MODEL), BF16),
                   jax.ShapeDtypeStruct((B, S // Q_BLK, N_HEADS * HEAD_PAD, Q_BLK), BF16),
                   jax.ShapeDtypeStruct((B, S, N_HEADS * HEAD_PAD), BF16),
                   jax.ShapeDtypeStruct((B, N_HEADS * V_DIM, S), BF16)],
        compiler_params=_params(("arbitrary", "arbitrary")),
        name="inproj",
    )(x, mod3, tabq, tabk, g_mix, g_q, g_kv, wglu, wgate, wlat, wq1t, wkn, wvt)


def _ctxproj_kernel(c_ref, mod_ref, gmix_ref, gkv_ref, wc_ref, wkn_ref, wvt_ref, k_ref, vt_ref):
    m = mod_ref[0]
    shift, scale = m[:, 0:D_MODEL], m[:, D_MODEL:2 * D_MODEL]
    hc = (_rms(c_ref[0], gmix_ref[...]) * (1.0 + scale) + shift).astype(BF16)
    z = _dot(hc, wc_ref[...])
    kvn = _rms(z[:, 0:KV_LORA], gkv_ref[...]).astype(BF16)
    _store_k(k_ref, _dot(kvn, wkn_ref[...]), z[:, KV_LORA:KV_LORA + LANES])
    vt_ref[0] = _dot_nt(wvt_ref[...], kvn).astype(BF16)


def _ctxproj_call(ctx, mod3, g_mix, g_kv, wc, wkn, wvt, ctx_row):
    B, C, _ = ctx.shape
    return pl.pallas_call(
        _ctxproj_kernel,
        grid=(B,),
        in_specs=[pl.BlockSpec((1, C, D_MODEL), lambda b: (b, 0, 0)),
                  pl.BlockSpec((1, 1, 6 * D_MODEL), lambda b: (ctx_row, 0, 0)),
                  _resident((1, D_MODEL)), _resident((1, KV_LORA)),
                  _resident(wc.shape), _resident(wkn.shape), _resident(wvt.shape)],
        out_specs=[pl.BlockSpec((1, C, N_HEADS * HEAD_PAD), lambda b: (b, 0, 0)),
                   pl.BlockSpec((1, N_HEADS * V_DIM, C), lambda b: (b, 0, 0))],
        out_shape=[jax.ShapeDtypeStruct((B, C, N_HEADS * HEAD_PAD), BF16),
                   jax.ShapeDtypeStruct((B, N_HEADS * V_DIM, C), BF16)],
        compiler_params=_params(("arbitrary",)),
        name="ctxproj",
    )(ctx, mod3, g_mix, g_kv, wc, wkn, wvt)


def _attn_kernel(qt_ref, kx_ref, kc_ref, vxt_ref, vct_ref, o_ref, s_ref):
    nq = qt_ref.shape[1]
    n_ctx = kc_ref.shape[1] // K_BLK
    nch = n_ctx + kx_ref.shape[1] // K_BLK

    def k_chunk(h, c):
        cols = slice(h * HEAD_PAD, (h + 1) * HEAD_PAD)
        if c < n_ctx:
            return kc_ref[0, c * K_BLK:(c + 1) * K_BLK, cols]
        return kx_ref[0, (c - n_ctx) * K_BLK:(c - n_ctx + 1) * K_BLK, cols]

    def vt_chunk(h, c):
        rows = slice(h * V_DIM, (h + 1) * V_DIM)
        if c < n_ctx:
            return vct_ref[0, rows, c * K_BLK:(c + 1) * K_BLK]
        return vxt_ref[0, rows, (c - n_ctx) * K_BLK:(c - n_ctx + 1) * K_BLK]

    def fold8(x, op):
        return op(x.reshape(K_BLK // 8, 8, Q_BLK), axis=0)

    items = [(h * nq + j, c) for h in range(ATTN_HEADS) for j in range(nq) for c in range(nch)]
    lead = nch + ATTN_LAG
    mxs = {}

    def scores(t):
        blk, c = items[t]
        h, j = divmod(blk, nq)
        qt = qt_ref[0, j, h * HEAD_PAD:(h + 1) * HEAD_PAD, :]
        s = _dot(k_chunk(h, c), qt)
        s_ref[blk % 2, c] = s
        r = fold8(s, jnp.max)
        mxs[blk] = r if blk not in mxs else jnp.maximum(mxs[blk], r)

    for t in range(min(lead, len(items))):
        scores(t)
    for g, (blk, c) in enumerate(items):
        h, j = divmod(blk, nq)
        if c == 0:
            m = jnp.max(mxs.pop(blk), axis=0, keepdims=True)
            l8, ot = None, None
        if g + lead < len(items):
            scores(g + lead)
        p = jnp.exp2(s_ref[blk % 2, c] - m)
        ps = fold8(p, jnp.sum)
        l8 = ps if l8 is None else l8 + ps
        d = _dot(vt_chunk(h, c), p.astype(BF16))
        ot = d if ot is None else ot + d
        if c == nch - 1:
            l = jnp.sum(l8, axis=0, keepdims=True)
            o_ref[0, j * Q_BLK:(j + 1) * Q_BLK, h * V_DIM:(h + 1) * V_DIM] = (
                (ot * (1.0 / l)).T.astype(o_ref.dtype))


def _attn_call(qt, kx, kc, vxt, vct):
    B, S, _ = kx.shape
    C = kc.shape[1]
    assert C % K_BLK == 0 and S % K_BLK == 0 and S % Q_BLK == 0
    nh = ATTN_HEADS
    assert N_HEADS % nh == 0
    return pl.pallas_call(
        _attn_kernel,
        grid=(B, N_HEADS // nh),
        in_specs=[pl.BlockSpec((1, S // Q_BLK, nh * HEAD_PAD, Q_BLK), lambda b, h: (b, 0, h, 0)),
                  pl.BlockSpec((1, S, nh * HEAD_PAD), lambda b, h: (b, 0, h)),
                  pl.BlockSpec((1, C, nh * HEAD_PAD), lambda b, h: (b, 0, h)),
                  pl.BlockSpec((1, nh * V_DIM, S), lambda b, h: (b, h, 0)),
                  pl.BlockSpec((1, nh * V_DIM, C), lambda b, h: (b, h, 0))],
        out_specs=pl.BlockSpec((1, S, nh * V_DIM), lambda b, h: (b, 0, h)),
        out_shape=jax.ShapeDtypeStruct((B, S, N_HEADS * V_DIM), BF16),
        scratch_shapes=[pltpu.VMEM((2, (C + S) // K_BLK, K_BLK, Q_BLK), F32)],
        compiler_params=_params(("arbitrary", "arbitrary")),
        name="attn",
    )(qt, kx, kc, vxt, vct)


def _tail_kernel(u_ref, att_ref, gate_ref, x_ref, mod_ref, w3_ref, b3_ref, lg_ref, lb_ref,
                 gffn_ref, gfin_ref, wpw_ref, wo_ref, wout_ref, w13_ref, w2_ref,
                 o_ref, ring_ref, pk_ref, y_ref, cv_ref, mg_ref, x1_ref, h2_ref, h_ref,
                 *, n_seq, n_tiles, cn):
    tm = x_ref.shape[1]
    d_ff = w2_ref.shape[0]
    sub = D_CONV // LANES
    g = pl.program_id(0)

    @pl.when(g == 0)
    def _():
        ring_ref[...] = jnp.zeros(ring_ref.shape, F32)
        cv_ref[...] = jnp.zeros(cv_ref.shape, BF16)

    def ingest():
        pos = lax.rem(g, n_seq)
        keep_prev = jnp.where(jnp.logical_or(pos == 0, g >= n_tiles), 0.0, 1.0).astype(F32)
        keep_next = jnp.where(pos == n_seq - 1, 0.0, 1.0).astype(F32)
        slot, nxt = lax.rem(g, CONV_RING), lax.rem(g + 1, CONV_RING)
        prv = lax.rem(g + CONV_RING - 1, CONV_RING)
        for s in range(sub):
            col = u_ref[0, :, s * LANES:(s + 1) * LANES]
            ring_ref[slot, pl.ds(CONV_HALO * sub + s, tm, stride=sub), :] = col
            ring_ref[prv, pl.ds((CONV_HALO + tm) * sub + s, CONV_HALO, stride=sub), :] = (
                col[0:CONV_HALO] * keep_prev)
            ring_ref[nxt, pl.ds(s, CONV_HALO, stride=sub), :] = col[tm - CONV_HALO:tm] * keep_next

    cslot = lax.rem(g + 2, CONV_RING)
    cpar = lax.rem(g, 2)
    bias, lg, lb = b3_ref[...], lg_ref[...], lb_ref[...]
    off = CONV_HALO - CONV_WIDTH // 2

    half = tm // 2
    for jt in range(half + 2 * CONV_HALO):
        lo = ring_ref[cslot, jt * sub:(jt + 1) * sub, :]
        hi = ring_ref[cslot, (jt + half) * sub:(jt + half + 1) * sub, :]
        pk_ref[jt * 2 * sub:(jt + 1) * 2 * sub, :] = jnp.concatenate([lo, hi], axis=0).astype(BF16)

    def conv_chunk(r, gate):
        acc = jnp.zeros((CONV_ROWS, 2 * sub, LANES), BF16)
        for k in range(CONV_WIDTH):
            t0 = (r * CONV_ROWS + off + k) * 2 * sub
            win = pk_ref[t0:t0 + CONV_ROWS * 2 * sub, :]
            wk = w3_ref[k]
            if k == 0 and gate is not None:
                wk = wk + jnp.concatenate([gate, gate], axis=0).astype(BF16)
            acc = acc + win.reshape(CONV_ROWS, 2 * sub, LANES) * wk
        accf = acc.astype(F32)
        for hb in range(2):
            yh = accf[:, hb * sub:(hb + 1) * sub, :] + bias
            y_ref[2 * r + hb] = yh.reshape(CONV_ROWS * sub, LANES)
            y = jnp.concatenate([y_ref[2 * r + hb, pl.ds(s, CONV_ROWS, stride=sub), :]
                                 for s in range(sub)], axis=-1)
            mu = jnp.mean(y, axis=-1, keepdims=True)
            yc = y - mu
            var = jnp.mean(yc * yc, axis=-1, keepdims=True)
            y = yc * lax.rsqrt(var + EPS) * lg + lb
            out = _silu(y)
            row0 = hb * half + r * CONV_ROWS
            cv_ref[cpar, row0:row0 + CONV_ROWS, :] = out.astype(BF16)
        return _zero_of(out[0:8, 0:LANES])

    m = mod_ref[0]
    gate_mix = m[:, 2 * D_MODEL:3 * D_MODEL]
    shift, scale = m[:, 3 * D_MODEL:4 * D_MODEL], m[:, 4 * D_MODEL:5 * D_MODEL]
    gate_ffn = m[:, 5 * D_MODEL:6 * D_MODEL]
    st = {}

    def anchor(ref, top, tok):
        if tok is not None:
            ref[0:16, 0:LANES] = (top + jnp.concatenate([tok, tok], axis=0)).astype(BF16)

    def merge_chunk(j, tok):
        del tok
        y_conv = _dot(cv_ref[1 - cpar], wpw_ref[:, j:j + cn])
        y_mla = _dot(att_ref[0], wo_ref[:, j:j + cn])
        merged = (gate_ref[0, :, j:j + cn].astype(F32) * y_conv
                  + gate_ref[0, :, D_MODEL + j:D_MODEL + j + cn].astype(F32) * y_mla)
        if j == 0:
            st["mg_top"] = merged[0:16, 0:LANES]
        mg_ref[:, j:j + cn] = merged.astype(BF16)
        return _zero_of(y_conv[0:8, 0:LANES])

    def out_chunk(j, tok):
        anchor(mg_ref, st["mg_top"], tok)
        yo = _dot(mg_ref[...], wout_ref[:, j:j + cn])
        x1_ref[:, j:j + cn] = x_ref[0, :, j:j + cn] + gate_mix[:, j:j + cn] * yo
        return _zero_of(yo[0:8, 0:LANES])

    def ffn_chunk(j, tok):
        if j == 0:
            h2 = _rms(x1_ref[...], gffn_ref[...]) * (1.0 + scale) + shift
            st["h2_top"] = h2[0:16, 0:LANES]
            h2_ref[...] = h2.astype(BF16)
        anchor(h2_ref, st["h2_top"], tok)
        w_ab = jnp.concatenate([w13_ref[:, j:j + LANES], w13_ref[:, d_ff + j:d_ff + j + LANES]], axis=1)
        ab = _dot(h2_ref[...], w_ab)
        hv = _silu(ab[:, 0:LANES]) * ab[:, LANES:]
        if j == 0:
            st["h_top"] = hv[0:16, :]
        h_ref[:, j:j + LANES] = hv.astype(BF16)
        return _zero_of(ab[0:8, 0:LANES])

    def down_chunk(j, tok):
        anchor(h_ref, st["h_top"], tok)
        yd = _dot(h_ref[...], w2_ref[:, j:j + cn])
        x1_ref[:, j:j + cn] = x1_ref[:, j:j + cn] + gate_ffn[:, j:j + cn] * yd
        return _zero_of(yd[0:8, 0:LANES])

    cols = range(0, D_MODEL, cn)
    mxu_tasks = ([functools.partial(merge_chunk, j) for j in cols]
                 + [functools.partial(out_chunk, j) for j in cols]
                 + [functools.partial(ffn_chunk, j) for j in range(0, d_ff, LANES)]
                 + [functools.partial(down_chunk, j) for j in cols])
    cost = ([0] + [2 * D_MODEL * cn] * (len(cols) - 1) + [D_MODEL * cn] * len(cols)
            + [2 * D_MODEL * LANES] * (d_ff // LANES) + [d_ff * cn] * len(cols))
    n_conv = half // CONV_ROWS
    done = 0
    mxu_toks, group_toks = [None] * TAIL_LAG, [None] * TAIL_LAG
    for i, task in enumerate(mxu_tasks):
        want = (sum(cost[:i + 1]) * n_conv) // sum(cost)
        group = None
        for r in range(done, want):
            tok = conv_chunk(r, mxu_toks[-TAIL_LAG])
            group = tok if group is None else group + tok
        done = want
        mxu_toks.append(task(group_toks[-TAIL_LAG]))
        group_toks.append(group)
    o_ref[0] = _rms(x1_ref[...], gfin_ref[...])
    ingest()


def _tail_call(u, att, gates, x, mod3, w3, b3, ln_g, ln_b, g_ffn, g_final,
               w_pw, w_o, w_out, w13, w2, tm, cn):
    B, S, _ = x.shape
    n_seq = S // tm
    n_tiles = B * n_seq
    d_ff = w2.shape[0]
    sub = D_CONV // LANES
    assert tm % CONV_ROWS == 0 and d_ff % cn == 0

    def tile(t):
        return (t // n_seq, lax.rem(t, n_seq), 0)

    in_tile = lambda g: tile(jnp.minimum(g, n_tiles - 1))
    out_t = lambda g: jnp.clip(g - 3, 0, n_tiles - 1)
    out_tile = lambda g: tile(out_t(g))
    row = lambda w, f: pl.BlockSpec((1, tm, w), f)
    return pl.pallas_call(
        functools.partial(_tail_kernel, n_seq=n_seq, n_tiles=n_tiles, cn=cn),
        grid=(n_tiles + 3,),
        in_specs=[row(D_CONV, in_tile), row(N_HEADS * V_DIM, out_tile), row(2 * D_MODEL, out_tile),
                  row(D_MODEL, out_tile),
                  pl.BlockSpec((1, 1, 6 * D_MODEL), lambda g: (out_t(g) // n_seq, 0, 0)),
                  _resident(w3.shape), _resident(b3.shape), _resident((1, D_CONV)),
                  _resident((1, D_CONV)), _resident((1, D_MODEL)), _resident((1, D_MODEL)),
                  _resident(w_pw.shape), _resident(w_o.shape), _resident(w_out.shape),
                  _resident(w13.shape), _resident(w2.shape)],
        out_specs=row(D_MODEL, out_tile),
        out_shape=jax.ShapeDtypeStruct((B, S, D_MODEL), F32),
        scratch_shapes=[pltpu.VMEM((CONV_RING, (tm + 2 * CONV_HALO) * sub, LANES), F32),
                        pltpu.VMEM(((tm // 2 + 2 * CONV_HALO) * 2 * sub, LANES), BF16),
                        pltpu.VMEM((tm // CONV_ROWS, CONV_ROWS * sub, LANES), F32),
                        pltpu.VMEM((2, tm, D_CONV), BF16),
                        pltpu.VMEM((tm, D_MODEL), BF16),
                        pltpu.VMEM((tm, D_MODEL), F32),
                        pltpu.VMEM((tm, D_MODEL), BF16),
                        pltpu.VMEM((tm, d_ff), BF16)],
        compiler_params=_params(("arbitrary",)),
        name="tail",
    )(u, att, gates, x, mod3, w3, b3, ln_g, ln_b, g_ffn, g_final, w_pw, w_o, w_out, w13, w2)


def _rope_tables(S):
    f32 = np.float32
    rows = S // GRID_W
    row = np.repeat(np.arange(rows, dtype=f32), GRID_W)
    col = np.tile(np.arange(GRID_W, dtype=f32), rows)
    axis_dim = QK_ROPE // 2
    inv_freq = (f32(ROPE_THETA) ** (-np.arange(0, axis_dim, 2, dtype=f32) / f32(axis_dim))).astype(f32)
    ar, ac = row[:, None] * inv_freq, col[:, None] * inv_freq
    cos = np.concatenate([np.cos(ar), np.cos(ar), np.cos(ac), np.cos(ac)], axis=-1).astype(f32)
    sin = np.concatenate([-np.sin(ar), np.sin(ar), -np.sin(ac), np.sin(ac)], axis=-1).astype(f32)
    z = np.zeros((S, LANES - QK_ROPE), f32)
    tabq = np.ascontiguousarray((f32(Q_SCALE) * np.concatenate([cos, z, sin, z], axis=-1)).T)
    tabk = np.concatenate([cos, sin], axis=-1)
    return jnp.asarray(tabq), jnp.asarray(tabk)


_ROPE_SWAP = np.concatenate([np.arange(16, 32), np.arange(0, 16),
                             np.arange(48, 64), np.arange(32, 48)])


def _prep_weights(w_in, w_uq, w_ukv):
    off_q, off_kv = 2 * D_CONV, 2 * D_CONV + Q_LORA
    off_kr, off_gate = off_kv + KV_LORA, off_kv + KV_LORA + QK_ROPE
    w_kr = w_in[:, off_kr:off_gate]
    z64 = jnp.zeros((D_MODEL, LANES - QK_ROPE), w_in.dtype)
    wglu = w_in[:, 0:off_q].astype(BF16)
    wgate = w_in[:, off_gate:].astype(BF16)
    wlat = jnp.concatenate([w_in[:, off_q:off_kr], w_kr, w_kr[:, _ROPE_SWAP]],
                           axis=1).astype(BF16)
    wc = jnp.concatenate([w_in[:, off_kv:off_gate], z64], axis=1).astype(BF16)

    uq = w_uq.reshape(Q_LORA, N_HEADS, QK_NOPE + QK_ROPE)
    nope, rope = uq[:, :, :QK_NOPE], uq[:, :, QK_NOPE:]
    zq = jnp.zeros((Q_LORA, N_HEADS, LANES - QK_ROPE), w_uq.dtype)
    wq1t = jnp.concatenate([nope, rope, zq], axis=2).reshape(Q_LORA, N_HEADS * HEAD_PAD).T.astype(BF16)

    ukv = w_ukv.reshape(KV_LORA, N_HEADS, QK_NOPE + V_DIM)
    wkn = ukv[:, :, :QK_NOPE].reshape(KV_LORA, N_HEADS * QK_NOPE).astype(BF16)
    wvt = ukv[:, :, QK_NOPE:].reshape(KV_LORA, N_HEADS * V_DIM).T.astype(BF16)
    return wglu, wgate, wlat, wc, wq1t, wkn, wvt


def kernel(x, c, ctx, c_ctx, w_mod, b_mod, g_mix, g_ffn, w_in, g_q, g_kv, w_uq, w_ukv,
           w_o_mla, w_dw, b_dw, ln_g, ln_b, w_pw, w_out, w_13, w_2, g_final):
    B, S, _ = x.shape
    assert w_mod.shape[0] == 1, "single-layer block"
    assert B + 1 <= MOD_ROWS

    wglu, wgate, wlat, wc, wq1t, wkn, wvt = _prep_weights(w_in[0], w_uq[0], w_ukv[0])
    tabq, tabk = _rope_tables(S)
    w3 = w_dw[0].reshape(CONV_WIDTH, D_CONV // LANES, LANES)
    w3 = jnp.concatenate([w3, w3], axis=1).astype(BF16)
    b3 = b_dw.reshape(D_CONV // LANES, LANES)

    cc = jnp.zeros((MOD_ROWS, D_MODEL), F32).at[0:B].set(c).at[B].set(c_ctx)
    mod3 = _mod_call(cc, w_mod[0], b_mod).reshape(MOD_ROWS, 1, 6 * D_MODEL)

    u, gates, qt, kx, vxt = _inproj_call(x, mod3, tabq, tabk, g_mix, g_q, g_kv,
                                         wglu, wgate, wlat, wq1t, wkn, wvt, tm=256)
    kc, vct = _ctxproj_call(ctx, mod3, g_mix, g_kv, wc, wkn, wvt, ctx_row=B)
    att = _attn_call(qt, kx, kc, vxt, vct)
    return _tail_call(u, att, gates, x, mod3, w3, b3, ln_g, ln_b, g_ffn, g_final.reshape(1, D_MODEL),
                      w_pw[0].astype(BF16), w_o_mla[0].astype(BF16), w_out[0].astype(BF16),
                      w_13[0].astype(BF16), w_2[0].astype(BF16), tm=256, cn=256)
```

```python
import functools

import jax
import jax.numpy as jnp
import numpy as np
from jax import lax
from jax.experimental import pallas as pl
from jax.experimental.pallas import tpu as pltpu

F32 = jnp.float32
BF16 = jnp.bfloat16

D_MODEL = 1024
N_HEADS = 8
QK_NOPE = 128
QK_ROPE = 64
V_DIM = 128
Q_LORA = 384
KV_LORA = 256
D_CONV = 1024
CONV_WIDTH = 31
CONV_HALO = 16
GRID_W = 64
ROPE_THETA = 10000.0
EPS = 1e-6
ATTN_SCALE = (QK_NOPE + QK_ROPE) ** -0.5
Q_SCALE = ATTN_SCALE * float(np.log2(np.e))
HEAD_PAD = 256
Q_BLK = 256
K_BLK = 256
ATTN_LAG = 8
ATTN_SLOTS = 2
ATTN_HEADS = 2
CONV_ROWS = 8
CONV_RING = 4
INPROJ_COLS = 512
TAIL_LAG = 6
LANES = 128
MOD_ROWS = 24

VMEM_LIMIT = 56 * 1024 * 1024

_KR0 = Q_LORA + KV_LORA

_NT = (((1,), (1,)), ((), ()))


def _resident(shape):
    nd = len(shape)
    return pl.BlockSpec(shape, lambda *_: (0,) * nd, pipeline_mode=pl.Buffered(1))


def _params(sem):
    return pltpu.CompilerParams(dimension_semantics=sem, vmem_limit_bytes=VMEM_LIMIT)


def _rms(x, g):
    return x * lax.rsqrt(jnp.mean(x * x, axis=-1, keepdims=True) + EPS) * g


def _silu(x):
    return x * jax.nn.sigmoid(x)


def _dot(a, b):
    return jnp.dot(a, b, preferred_element_type=F32)


def _dot_nt(a, b):
    return lax.dot_general(a, b, _NT, preferred_element_type=F32)


def _zero_of(x):
    u = pltpu.bitcast(x, jnp.uint32)
    u = lax.shift_right_logical(lax.shift_right_logical(u, jnp.uint32(16)), jnp.uint32(16))
    return pltpu.bitcast(u, F32)


def _mod_kernel(cc_ref, w_ref, b_ref, o_ref):
    s = _silu(cc_ref[...])
    o_ref[...] = jnp.dot(s, w_ref[...], preferred_element_type=F32,
                         precision=lax.Precision.HIGHEST) + b_ref[...]


def _mod_call(cc, w_mod, b_mod):
    n = w_mod.shape[1]
    tn = 1024
    return pl.pallas_call(
        _mod_kernel,
        grid=(n // tn,),
        in_specs=[pl.BlockSpec((MOD_ROWS, D_MODEL), lambda j: (0, 0)),
                  pl.BlockSpec((D_MODEL, tn), lambda j: (0, j)),
                  pl.BlockSpec((1, tn), lambda j: (0, j))],
        out_specs=pl.BlockSpec((MOD_ROWS, tn), lambda j: (0, j)),
        out_shape=jax.ShapeDtypeStruct((MOD_ROWS, n), F32),
        compiler_params=_params(("arbitrary",)),
        name="mod",
    )(cc, w_mod, b_mod)


def _store_k(k_ref, kn, kr):
    krb = kr.astype(BF16)
    for h in range(N_HEADS):
        k_ref[0, :, h * HEAD_PAD:h * HEAD_PAD + QK_NOPE] = (
            kn[:, h * QK_NOPE:(h + 1) * QK_NOPE].astype(BF16))
        k_ref[0, :, h * HEAD_PAD + QK_NOPE:(h + 1) * HEAD_PAD] = krb


def _inproj_kernel(x_ref, mod_ref, tabq_ref, tabk_ref, gmix_ref, gq_ref, gkv_ref,
                   wglu_ref, wgate_ref, wlat_ref, wq1t_ref, wkn_ref, wvt_ref,
                   u_ref, gate_ref, qt_ref, k_ref, vt_ref):
    m = mod_ref[0]
    shift, scale = m[:, 0:D_MODEL], m[:, D_MODEL:2 * D_MODEL]
    hx = (_rms(x_ref[0], gmix_ref[...]) * (1.0 + scale) + shift).astype(BF16)

    cn = INPROJ_COLS

    def glu(j):
        a = _dot(hx, wglu_ref[:, j:j + cn])
        g = _dot(hx, wglu_ref[:, D_CONV + j:D_CONV + j + cn])
        u_ref[0, :, j:j + cn] = a * jax.nn.sigmoid(g)

    def gates(j):
        gate_ref[0, :, j:j + cn] = jax.nn.sigmoid(_dot(hx, wgate_ref[:, j:j + cn])).astype(BF16)

    wide = ([functools.partial(glu, j) for j in range(0, D_CONV, cn)]
            + [functools.partial(gates, j) for j in range(0, 2 * D_MODEL, cn)])
    n_wide = len(wide)

    def run_wide(upto):
        while len(wide) > n_wide - upto:
            wide.pop(0)()

    z = _dot(hx, wlat_ref[...])
    run_wide(n_wide // 6)
    qa = z[:, 0:Q_LORA]
    kva = z[:, Q_LORA:Q_LORA + KV_LORA]
    qn = _rms(qa, gq_ref[...]).astype(BF16)
    kvn = _rms(kva, gkv_ref[...]).astype(BF16)
    qf = _dot_nt(wq1t_ref[...], qn)
    kn = _dot(kvn, wkn_ref[...])
    vt = _dot_nt(wvt_ref[...], kvn)
    run_wide(n_wide // 2)

    kr_prod = z[:, _KR0:_KR0 + LANES] * tabk_ref[...]
    kr_sum = kr_prod + pltpu.roll(kr_prod, LANES // 2, axis=1)
    lane = lax.broadcasted_iota(jnp.int32, (1, LANES), 1)
    _store_k(k_ref, kn, jnp.where(lane < QK_ROPE, kr_sum, 0.0))
    vt_ref[0] = vt.astype(BF16)

    tabq = tabq_ref[...]
    tq_c, tq_s = tabq[0:LANES], tabq[LANES:2 * LANES]
    half = QK_ROPE // 4
    for h in range(N_HEADS):
        c0 = h * HEAD_PAD
        q_nope = (qf[c0:c0 + QK_NOPE] * Q_SCALE).astype(BF16)
        qr = qf[c0 + QK_NOPE:c0 + HEAD_PAD]
        qr_swap = jnp.concatenate([qr[half:2 * half], qr[0:half], qr[3 * half:4 * half],
                                   qr[2 * half:3 * half], qr[4 * half:]], axis=0)
        q_rope = (qr * tq_c + qr_swap * tq_s).astype(BF16)
        for jb in range(qt_ref.shape[1]):
            qt_ref[0, jb, c0:c0 + QK_NOPE, :] = q_nope[:, jb * Q_BLK:(jb + 1) * Q_BLK]
            qt_ref[0, jb, c0 + QK_NOPE:c0 + HEAD_PAD, :] = q_rope[:, jb * Q_BLK:(jb + 1) * Q_BLK]
        run_wide(n_wide // 2 + ((h + 1) * (n_wide // 3)) // N_HEADS)
    run_wide(n_wide)


def _inproj_call(x, mod3, tabq, tabk, g_mix, g_q, g_kv, wglu, wgate, wlat, wq1t, wkn, wvt, tm):
    B, S, _ = x.shape
    row = lambda w: pl.BlockSpec((1, tm, w), lambda b, i: (b, i, 0))
    col = lambda w: pl.BlockSpec((1, w, tm), lambda b, i: (b, 0, i))
    return pl.pallas_call(
        _inproj_kernel,
        grid=(B, S // tm),
        in_specs=[row(D_MODEL),
                  pl.BlockSpec((1, 1, 6 * D_MODEL), lambda b, i: (b, 0, 0)),
                  pl.BlockSpec((2 * LANES, tm), lambda b, i: (0, i)),
                  pl.BlockSpec((tm, LANES), lambda b, i: (i, 0)),
                  _resident((1, D_MODEL)), _resident((1, Q_LORA)), _resident((1, KV_LORA)),
                  _resident(wglu.shape), _resident(wgate.shape), _resident(wlat.shape),
                  _resident(wq1t.shape),
                  _resident(wkn.shape), _resident(wvt.shape)],
        out_specs=[row(D_CONV), row(2 * D_MODEL),
                   pl.BlockSpec((1, tm // Q_BLK, N_HEADS * HEAD_PAD, Q_BLK), lambda b, i: (b, i, 0, 0)),
                   row(N_HEADS * HEAD_PAD), col(N_HEADS * V_DIM)],
        out_shape=[jax.ShapeDtypeStruct((B, S, D_CONV), F32),
                   jax.ShapeDtypeStruct((B, S, 2 * D_MODEL), BF16),
                   jax.ShapeDtypeStruct((B, S // Q_BLK, N_HEADS * HEAD_PAD, Q_BLK), BF16),
                   jax.ShapeDtypeStruct((B, S, N_HEADS * HEAD_PAD), BF16),
                   jax.ShapeDtypeStruct((B, N_HEADS * V_DIM, S), BF16)],
        compiler_params=_params(("arbitrary", "arbitrary")),
        name="inproj",
    )(x, mod3, tabq, tabk, g_mix, g_q, g_kv, wglu, wgate, wlat, wq1t, wkn, wvt)


def _ctxproj_kernel(c_ref, mod_ref, gmix_ref, gkv_ref, wc_ref, wkn_ref, wvt_ref, k_ref, vt_ref):
    m = mod_ref[0]
    shift, scale = m[:, 0:D_MODEL], m[:, D_MODEL:2 * D_MODEL]
    hc = (_rms(c_ref[0], gmix_ref[...]) * (1.0 + scale) + shift).astype(BF16)
    z = _dot(hc, wc_ref[...])
    kvn = _rms(z[:, 0:KV_LORA], gkv_ref[...]).astype(BF16)
    _store_k(k_ref, _dot(kvn, wkn_ref[...]), z[:, KV_LORA:KV_LORA + LANES])
    vt_ref[0] = _dot_nt(wvt_ref[...], kvn).astype(BF16)


def _ctxproj_call(ctx, mod3, g_mix, g_kv, wc, wkn, wvt, ctx_row):
    B, C, _ = ctx.shape
    return pl.pallas_call(
        _ctxproj_kernel,
        grid=(B,),
        in_specs=[pl.BlockSpec((1, C, D_MODEL), lambda b: (b, 0, 0)),
                  pl.BlockSpec((1, 1, 6 * D_MODEL), lambda b: (ctx_row, 0, 0)),
                  _resident((1, D_MODEL)), _resident((1, KV_LORA)),
                  _resident(wc.shape), _resident(wkn.shape), _resident(wvt.shape)],
        out_specs=[pl.BlockSpec((1, C, N_HEADS * HEAD_PAD), lambda b: (b, 0, 0)),
                   pl.BlockSpec((1, N_HEADS * V_DIM, C), lambda b: (b, 0, 0))],
        out_shape=[jax.ShapeDtypeStruct((B, C, N_HEADS * HEAD_PAD), BF16),
                   jax.ShapeDtypeStruct((B, N_HEADS * V_DIM, C), BF16)],
        compiler_params=_params(("arbitrary",)),
        name="ctxproj",
    )(ctx, mod3, g_mix, g_kv, wc, wkn, wvt)


def _attn_kernel(qt_ref, kx_ref, kc_ref, vxt_ref, vct_ref, o_ref, s_ref):
    nq = qt_ref.shape[1]
    n_ctx = kc_ref.shape[1] // K_BLK
    nch = n_ctx + kx_ref.shape[1] // K_BLK

    def k_chunk(h, c):
        cols = slice(h * HEAD_PAD, (h + 1) * HEAD_PAD)
        if c < n_ctx:
            return kc_ref[0, c * K_BLK:(c + 1) * K_BLK, cols]
        return kx_ref[0, (c - n_ctx) * K_BLK:(c - n_ctx + 1) * K_BLK, cols]

    def vt_chunk(h, c):
        rows = slice(h * V_DIM, (h + 1) * V_DIM)
        if c < n_ctx:
            return vct_ref[0, rows, c * K_BLK:(c + 1) * K_BLK]
        return vxt_ref[0, rows, (c - n_ctx) * K_BLK:(c - n_ctx + 1) * K_BLK]

    def fold8(x, op):
        return op(x.reshape(K_BLK // 8, 8, Q_BLK), axis=0)

    items = [(h * nq + j, c) for h in range(ATTN_HEADS) for j in range(nq) for c in range(nch)]
    lead = nch + ATTN_LAG
    n_slots = s_ref.shape[0]
    assert lead < n_slots * nch
    mxs = {}

    def scores(t):
        blk, c = items[t]
        h, j = divmod(blk, nq)
        qt = qt_ref[0, j, h * HEAD_PAD:(h + 1) * HEAD_PAD, :]
        s = _dot(k_chunk(h, c), qt)
        s_ref[blk % n_slots, c] = s
        r = fold8(s, jnp.max)
        mxs[blk] = r if blk not in mxs else jnp.maximum(mxs[blk], r)

    for t in range(min(lead, len(items))):
        scores(t)
    for g, (blk, c) in enumerate(items):
        h, j = divmod(blk, nq)
        if c == 0:
            m = jnp.max(mxs.pop(blk), axis=0, keepdims=True)
            l8, ot = None, None
        if g + lead < len(items):
            scores(g + lead)
        p = jnp.exp2(s_ref[blk % n_slots, c] - m)
        ps = fold8(p, jnp.sum)
        l8 = ps if l8 is None else l8 + ps
        d = _dot(vt_chunk(h, c), p.astype(BF16))
        ot = d if ot is None else ot + d
        if c == nch - 1:
            l = jnp.sum(l8, axis=0, keepdims=True)
            o_ref[0, j * Q_BLK:(j + 1) * Q_BLK, h * V_DIM:(h + 1) * V_DIM] = (
                (ot * (1.0 / l)).T.astype(o_ref.dtype))


def _attn_call(qt, kx, kc, vxt, vct):
    B, S, _ = kx.shape
    C = kc.shape[1]
    assert C % K_BLK == 0 and S % K_BLK == 0 and S % Q_BLK == 0
    nh = ATTN_HEADS
    assert N_HEADS % nh == 0
    return pl.pallas_call(
        _attn_kernel,
        grid=(B, N_HEADS // nh),
        in_specs=[pl.BlockSpec((1, S // Q_BLK, nh * HEAD_PAD, Q_BLK), lambda b, h: (b, 0, h, 0)),
                  pl.BlockSpec((1, S, nh * HEAD_PAD), lambda b, h: (b, 0, h)),
                  pl.BlockSpec((1, C, nh * HEAD_PAD), lambda b, h: (b, 0, h)),
                  pl.BlockSpec((1, nh * V_DIM, S), lambda b, h: (b, h, 0)),
                  pl.BlockSpec((1, nh * V_DIM, C), lambda b, h: (b, h, 0))],
        out_specs=pl.BlockSpec((1, S, nh * V_DIM), lambda b, h: (b, 0, h)),
        out_shape=jax.ShapeDtypeStruct((B, S, N_HEADS * V_DIM), BF16),
        scratch_shapes=[pltpu.VMEM((ATTN_SLOTS, (C + S) // K_BLK, K_BLK, Q_BLK), F32)],
        compiler_params=_params(("arbitrary", "arbitrary")),
        name="attn",
    )(qt, kx, kc, vxt, vct)


def _tail_kernel(u_ref, att_ref, gate_ref, x_ref, mod_ref, w3_ref, b3_ref, lg_ref, lb_ref,
                 gffn_ref, gfin_ref, wpw_ref, wo_ref, wout_ref, w13_ref, w2_ref,
                 o_ref, ring_ref, pk_ref, y_ref, cv_ref, mg_ref, x1_ref, h2_ref, h_ref,
                 *, n_seq, n_tiles, cn):
    tm = x_ref.shape[1]
    d_ff = w2_ref.shape[0]
    sub = D_CONV // LANES
    g = pl.program_id(0)

    @pl.when(g == 0)
    def _():
        ring_ref[...] = jnp.zeros(ring_ref.shape, F32)
        cv_ref[...] = jnp.zeros(cv_ref.shape, BF16)

    def ingest():
        pos = lax.rem(g, n_seq)
        keep_prev = jnp.where(jnp.logical_or(pos == 0, g >= n_tiles), 0.0, 1.0).astype(F32)
        keep_next = jnp.where(pos == n_seq - 1, 0.0, 1.0).astype(F32)
        slot, nxt = lax.rem(g, CONV_RING), lax.rem(g + 1, CONV_RING)
        prv = lax.rem(g + CONV_RING - 1, CONV_RING)
        for s in range(sub):
            col = u_ref[0, :, s * LANES:(s + 1) * LANES]
            ring_ref[slot, pl.ds(CONV_HALO * sub + s, tm, stride=sub), :] = col
            ring_ref[prv, pl.ds((CONV_HALO + tm) * sub + s, CONV_HALO, stride=sub), :] = (
                col[0:CONV_HALO] * keep_prev)
            ring_ref[nxt, pl.ds(s, CONV_HALO, stride=sub), :] = col[tm - CONV_HALO:tm] * keep_next

    cslot = lax.rem(g + 2, CONV_RING)
    cpar = lax.rem(g, 2)
    bias, lg, lb = b3_ref[...], lg_ref[...], lb_ref[...]
    off = CONV_HALO - CONV_WIDTH // 2

    half = tm // 2
    for jt in range(half + 2 * CONV_HALO):
        lo = ring_ref[cslot, jt * sub:(jt + 1) * sub, :]
        hi = ring_ref[cslot, (jt + half) * sub:(jt + half + 1) * sub, :]
        pk_ref[jt * 2 * sub:(jt + 1) * 2 * sub, :] = jnp.concatenate([lo, hi], axis=0).astype(BF16)

    def conv_chunk(r, gate):
        acc = jnp.zeros((CONV_ROWS, 2 * sub, LANES), BF16)
        for k in range(CONV_WIDTH):
            t0 = (r * CONV_ROWS + off + k) * 2 * sub
            win = pk_ref[t0:t0 + CONV_ROWS * 2 * sub, :]
            wk = w3_ref[k]
            if k == 0 and gate is not None:
                wk = wk + jnp.concatenate([gate, gate], axis=0).astype(BF16)
            acc = acc + win.reshape(CONV_ROWS, 2 * sub, LANES) * wk
        accf = acc.astype(F32)
        for hb in range(2):
            yh = accf[:, hb * sub:(hb + 1) * sub, :] + bias
            y_ref[2 * r + hb] = yh.reshape(CONV_ROWS * sub, LANES)
            y = jnp.concatenate([y_ref[2 * r + hb, pl.ds(s, CONV_ROWS, stride=sub), :]
                                 for s in range(sub)], axis=-1)
            mu = jnp.mean(y, axis=-1, keepdims=True)
            yc = y - mu
            var = jnp.mean(yc * yc, axis=-1, keepdims=True)
            y = yc * lax.rsqrt(var + EPS) * lg + lb
            out = _silu(y)
            row0 = hb * half + r * CONV_ROWS
            cv_ref[cpar, row0:row0 + CONV_ROWS, :] = out.astype(BF16)
        return _zero_of(out[0:8, 0:LANES])

    m = mod_ref[0]
    gate_mix = m[:, 2 * D_MODEL:3 * D_MODEL]
    shift, scale = m[:, 3 * D_MODEL:4 * D_MODEL], m[:, 4 * D_MODEL:5 * D_MODEL]
    gate_ffn = m[:, 5 * D_MODEL:6 * D_MODEL]
    st = {}

    def anchor(ref, top, tok):
        if tok is not None:
            ref[0:16, 0:LANES] = (top + jnp.concatenate([tok, tok], axis=0)).astype(BF16)

    def merge_chunk(j, tok):
        del tok
        y_conv = _dot(cv_ref[1 - cpar], wpw_ref[:, j:j + cn])
        y_mla = _dot(att_ref[0], wo_ref[:, j:j + cn])
        merged = (gate_ref[0, :, j:j + cn].astype(F32) * y_conv
                  + gate_ref[0, :, D_MODEL + j:D_MODEL + j + cn].astype(F32) * y_mla)
        if j == 0:
            st["mg_top"] = merged[0:16, 0:LANES]
        mg_ref[:, j:j + cn] = merged.astype(BF16)
        return _zero_of(y_conv[0:8, 0:LANES])

    def out_chunk(j, tok):
        anchor(mg_ref, st["mg_top"], tok)
        yo = _dot(mg_ref[...], wout_ref[:, j:j + cn])
        x1_ref[:, j:j + cn] = x_ref[0, :, j:j + cn] + gate_mix[:, j:j + cn] * yo
        return _zero_of(yo[0:8, 0:LANES])

    def ffn_chunk(j, tok):
        if j == 0:
            h2 = _rms(x1_ref[...], gffn_ref[...]) * (1.0 + scale) + shift
            st["h2_top"] = h2[0:16, 0:LANES]
            h2_ref[...] = h2.astype(BF16)
        anchor(h2_ref, st["h2_top"], tok)
        w_ab = jnp.concatenate([w13_ref[:, j:j + LANES], w13_ref[:, d_ff + j:d_ff + j + LANES]], axis=1)
        ab = _dot(h2_ref[...], w_ab)
        hv = _silu(ab[:, 0:LANES]) * ab[:, LANES:]
        if j == 0:
            st["h_top"] = hv[0:16, :]
        h_ref[:, j:j + LANES] = hv.astype(BF16)
        return _zero_of(ab[0:8, 0:LANES])

    def down_chunk(j, tok):
        anchor(h_ref, st["h_top"], tok)
        yd = _dot(h_ref[...], w2_ref[:, j:j + cn])
        x1_ref[:, j:j + cn] = x1_ref[:, j:j + cn] + gate_ffn[:, j:j + cn] * yd
        return _zero_of(yd[0:8, 0:LANES])

    cols = range(0, D_MODEL, cn)
    mxu_tasks = ([functools.partial(merge_chunk, j) for j in cols]
                 + [functools.partial(out_chunk, j) for j in cols]
                 + [functools.partial(ffn_chunk, j) for j in range(0, d_ff, LANES)]
                 + [functools.partial(down_chunk, j) for j in cols])
    cost = ([0] + [2 * D_MODEL * cn] * (len(cols) - 1) + [D_MODEL * cn] * len(cols)
            + [2 * D_MODEL * LANES] * (d_ff // LANES) + [d_ff * cn] * len(cols))
    n_conv = half // CONV_ROWS
    done = 0
    mxu_toks, group_toks = [None] * TAIL_LAG, [None] * TAIL_LAG
    for i, task in enumerate(mxu_tasks):
        want = (sum(cost[:i + 1]) * n_conv) // sum(cost)
        group = None
        for r in range(done, want):
            tok = conv_chunk(r, mxu_toks[-TAIL_LAG])
            group = tok if group is None else group + tok
        done = want
        mxu_toks.append(task(group_toks[-TAIL_LAG]))
        group_toks.append(group)
    o_ref[0] = _rms(x1_ref[...], gfin_ref[...])
    ingest()


def _tail_call(u, att, gates, x, mod3, w3, b3, ln_g, ln_b, g_ffn, g_final,
               w_pw, w_o, w_out, w13, w2, tm, cn):
    B, S, _ = x.shape
    n_seq = S // tm
    n_tiles = B * n_seq
    d_ff = w2.shape[0]
    sub = D_CONV // LANES
    assert tm % CONV_ROWS == 0 and d_ff % cn == 0

    def tile(t):
        return (t // n_seq, lax.rem(t, n_seq), 0)

    in_tile = lambda g: tile(jnp.minimum(g, n_tiles - 1))
    out_t = lambda g: jnp.clip(g - 3, 0, n_tiles - 1)
    out_tile = lambda g: tile(out_t(g))
    row = lambda w, f: pl.BlockSpec((1, tm, w), f)
    return pl.pallas_call(
        functools.partial(_tail_kernel, n_seq=n_seq, n_tiles=n_tiles, cn=cn),
        grid=(n_tiles + 3,),
        in_specs=[row(D_CONV, in_tile), row(N_HEADS * V_DIM, out_tile), row(2 * D_MODEL, out_tile),
                  row(D_MODEL, out_tile),
                  pl.BlockSpec((1, 1, 6 * D_MODEL), lambda g: (out_t(g) // n_seq, 0, 0)),
                  _resident(w3.shape), _resident(b3.shape), _resident((1, D_CONV)),
                  _resident((1, D_CONV)), _resident((1, D_MODEL)), _resident((1, D_MODEL)),
                  _resident(w_pw.shape), _resident(w_o.shape), _resident(w_out.shape),
                  _resident(w13.shape), _resident(w2.shape)],
        out_specs=row(D_MODEL, out_tile),
        out_shape=jax.ShapeDtypeStruct((B, S, D_MODEL), F32),
        scratch_shapes=[pltpu.VMEM((CONV_RING, (tm + 2 * CONV_HALO) * sub, LANES), F32),
                        pltpu.VMEM(((tm // 2 + 2 * CONV_HALO) * 2 * sub, LANES), BF16),
                        pltpu.VMEM((tm // CONV_ROWS, CONV_ROWS * sub, LANES), F32),
                        pltpu.VMEM((2, tm, D_CONV), BF16),
                        pltpu.VMEM((tm, D_MODEL), BF16),
                        pltpu.VMEM((tm, D_MODEL), F32),
                        pltpu.VMEM((tm, D_MODEL), BF16),
                        pltpu.VMEM((tm, d_ff), BF16)],
        compiler_params=_params(("arbitrary",)),
        name="tail",
    )(u, att, gates, x, mod3, w3, b3, ln_g, ln_b, g_ffn, g_final, w_pw, w_o, w_out, w13, w2)


def _rope_tables(S):
    f32 = np.float32
    rows = S // GRID_W
    row = np.repeat(np.arange(rows, dtype=f32), GRID_W)
    col = np.tile(np.arange(GRID_W, dtype=f32), rows)
    axis_dim = QK_ROPE // 2
    inv_freq = (f32(ROPE_THETA) ** (-np.arange(0, axis_dim, 2, dtype=f32) / f32(axis_dim))).astype(f32)
    ar, ac = row[:, None] * inv_freq, col[:, None] * inv_freq
    cos = np.concatenate([np.cos(ar), np.cos(ar), np.cos(ac), np.cos(ac)], axis=-1).astype(f32)
    sin = np.concatenate([-np.sin(ar), np.sin(ar), -np.sin(ac), np.sin(ac)], axis=-1).astype(f32)
    z = np.zeros((S, LANES - QK_ROPE), f32)
    tabq = np.ascontiguousarray((f32(Q_SCALE) * np.concatenate([cos, z, sin, z], axis=-1)).T)
    tabk = np.concatenate([cos, sin], axis=-1)
    return jnp.asarray(tabq), jnp.asarray(tabk)


_ROPE_SWAP = np.concatenate([np.arange(16, 32), np.arange(0, 16),
                             np.arange(48, 64), np.arange(32, 48)])


def _prep_weights(w_in, w_uq, w_ukv):
    off_q, off_kv = 2 * D_CONV, 2 * D_CONV + Q_LORA
    off_kr, off_gate = off_kv + KV_LORA, off_kv + KV_LORA + QK_ROPE
    w_kr = w_in[:, off_kr:off_gate]
    z64 = jnp.zeros((D_MODEL, LANES - QK_ROPE), w_in.dtype)
    wglu = w_in[:, 0:off_q].astype(BF16)
    wgate = w_in[:, off_gate:].astype(BF16)
    wlat = jnp.concatenate([w_in[:, off_q:off_kr], w_kr, w_kr[:, _ROPE_SWAP]],
                           axis=1).astype(BF16)
    wc = jnp.concatenate([w_in[:, off_kv:off_gate], z64], axis=1).astype(BF16)

    uq = w_uq.reshape(Q_LORA, N_HEADS, QK_NOPE + QK_ROPE)
    nope, rope = uq[:, :, :QK_NOPE], uq[:, :, QK_NOPE:]
    zq = jnp.zeros((Q_LORA, N_HEADS, LANES - QK_ROPE), w_uq.dtype)
    wq1t = jnp.concatenate([nope, rope, zq], axis=2).reshape(Q_LORA, N_HEADS * HEAD_PAD).T.astype(BF16)

    ukv = w_ukv.reshape(KV_LORA, N_HEADS, QK_NOPE + V_DIM)
    wkn = ukv[:, :, :QK_NOPE].reshape(KV_LORA, N_HEADS * QK_NOPE).astype(BF16)
    wvt = ukv[:, :, QK_NOPE:].reshape(KV_LORA, N_HEADS * V_DIM).T.astype(BF16)
    return wglu, wgate, wlat, wc, wq1t, wkn, wvt


def kernel(x, c, ctx, c_ctx, w_mod, b_mod, g_mix, g_ffn, w_in, g_q, g_kv, w_uq, w_ukv,
           w_o_mla, w_dw, b_dw, ln_g, ln_b, w_pw, w_out, w_13, w_2, g_final):
    B, S, _ = x.shape
    assert w_mod.shape[0] == 1, "single-layer block"
    assert B + 1 <= MOD_ROWS

    wglu, wgate, wlat, wc, wq1t, wkn, wvt = _prep_weights(w_in[0], w_uq[0], w_ukv[0])
    tabq, tabk = _rope_tables(S)
    w3 = w_dw[0].reshape(CONV_WIDTH, D_CONV // LANES, LANES)
    w3 = jnp.concatenate([w3, w3], axis=1).astype(BF16)
    b3 = b_dw.reshape(D_CONV // LANES, LANES)

    cc = jnp.zeros((MOD_ROWS, D_MODEL), F32).at[0:B].set(c).at[B].set(c_ctx)
    mod3 = _mod_call(cc, w_mod[0], b_mod).reshape(MOD_ROWS, 1, 6 * D_MODEL)

    u, gates, qt, kx, vxt = _inproj_call(x, mod3, tabq, tabk, g_mix, g_q, g_kv,
                                         wglu, wgate, wlat, wq1t, wkn, wvt, tm=256)
    kc, vct = _ctxproj_call(ctx, mod3, g_mix, g_kv, wc, wkn, wvt, ctx_row=B)
    att = _attn_call(qt, kx, kc, vxt, vct)
    return _tail_call(u, att, gates, x, mod3, w3, b3, ln_g, ln_b, g_ffn, g_final.reshape(1, D_MODEL),
                      w_pw[0].astype(BF16), w_o_mla[0].astype(BF16), w_out[0].astype(BF16),
                      w_13[0].astype(BF16), w_2[0].astype(BF16), tm=256, cn=256)
```

```python
import functools

import jax
import jax.numpy as jnp
import numpy as np
from jax import lax
from jax.experimental import pallas as pl
from jax.experimental.pallas import tpu as pltpu

F32 = jnp.float32
BF16 = jnp.bfloat16

D_MODEL = 1024
N_HEADS = 8
QK_NOPE = 128
QK_ROPE = 64
V_DIM = 128
Q_LORA = 384
KV_LORA = 256
D_CONV = 1024
CONV_WIDTH = 31
CONV_HALO = 16
GRID_W = 64
ROPE_THETA = 10000.0
EPS = 1e-6
ATTN_SCALE = (QK_NOPE + QK_ROPE) ** -0.5
Q_SCALE = ATTN_SCALE * float(np.log2(np.e))
HEAD_PAD = 256
Q_BLK = 256
K_BLK = 256
ATTN_LAG = 8
ATTN_SLOTS = 2
ATTN_HEADS = 2
CONV_ROWS = 8
CONV_RING = 4
INPROJ_COLS = 512
TAIL_LAG = 6
LANES = 128
MOD_ROWS = 24

VMEM_LIMIT = 56 * 1024 * 1024

_KR0 = Q_LORA + KV_LORA

_NT = (((1,), (1,)), ((), ()))


def _resident(shape):
    nd = len(shape)
    return pl.BlockSpec(shape, lambda *_: (0,) * nd, pipeline_mode=pl.Buffered(1))


def _params(sem):
    return pltpu.CompilerParams(dimension_semantics=sem, vmem_limit_bytes=VMEM_LIMIT)


def _rms(x, g):
    return x * lax.rsqrt(jnp.mean(x * x, axis=-1, keepdims=True) + EPS) * g


def _silu(x):
    return x * jax.nn.sigmoid(x)


def _dot(a, b):
    return jnp.dot(a, b, preferred_element_type=F32)


def _dot_nt(a, b):
    return lax.dot_general(a, b, _NT, preferred_element_type=F32)


def _zero_of(x):
    u = pltpu.bitcast(x, jnp.uint32)
    u = lax.shift_right_logical(lax.shift_right_logical(u, jnp.uint32(16)), jnp.uint32(16))
    return pltpu.bitcast(u, F32)


def _mod_kernel(cc_ref, w_ref, b_ref, o_ref):
    s = _silu(cc_ref[...])
    o_ref[...] = jnp.dot(s, w_ref[...], preferred_element_type=F32,
                         precision=lax.Precision.HIGHEST) + b_ref[...]


def _mod_call(cc, w_mod, b_mod):
    n = w_mod.shape[1]
    tn = 1024
    return pl.pallas_call(
        _mod_kernel,
        grid=(n // tn,),
        in_specs=[pl.BlockSpec((MOD_ROWS, D_MODEL), lambda j: (0, 0)),
                  pl.BlockSpec((D_MODEL, tn), lambda j: (0, j)),
                  pl.BlockSpec((1, tn), lambda j: (0, j))],
        out_specs=pl.BlockSpec((MOD_ROWS, tn), lambda j: (0, j)),
        out_shape=jax.ShapeDtypeStruct((MOD_ROWS, n), F32),
        compiler_params=_params(("arbitrary",)),
        name="mod",
    )(cc, w_mod, b_mod)


def _store_k(k_ref, kn, kr):
    krb = kr.astype(BF16)
    for h in range(N_HEADS):
        k_ref[0, :, h * HEAD_PAD:h * HEAD_PAD + QK_NOPE] = (
            kn[:, h * QK_NOPE:(h + 1) * QK_NOPE].astype(BF16))
        k_ref[0, :, h * HEAD_PAD + QK_NOPE:(h + 1) * HEAD_PAD] = krb


def _inproj_kernel(x_ref, mod_ref, tabq_ref, tabk_ref, gmix_ref, gq_ref, gkv_ref,
                   wglu_ref, wgate_ref, wlat_ref, wq1t_ref, wkn_ref, wvt_ref,
                   u_ref, gate_ref, qt_ref, k_ref, vt_ref):
    m = mod_ref[0]
    shift, scale = m[:, 0:D_MODEL], m[:, D_MODEL:2 * D_MODEL]
    hx = (_rms(x_ref[0], gmix_ref[...]) * (1.0 + scale) + shift).astype(BF16)

    cn = INPROJ_COLS

    def glu(j):
        a = _dot(hx, wglu_ref[:, j:j + cn])
        g = _dot(hx, wglu_ref[:, D_CONV + j:D_CONV + j + cn])
        u_ref[0, :, j:j + cn] = a * jax.nn.sigmoid(g)

    def gates(j):
        gate_ref[0, :, j:j + cn] = jax.nn.sigmoid(_dot(hx, wgate_ref[:, j:j + cn])).astype(BF16)

    wide = ([functools.partial(glu, j) for j in range(0, D_CONV, cn)]
            + [functools.partial(gates, j) for j in range(0, 2 * D_MODEL, cn)])
    n_wide = len(wide)

    def run_wide(upto):
        while len(wide) > n_wide - upto:
            wide.pop(0)()

    z = _dot(hx, wlat_ref[...])
    run_wide(n_wide // 6)
    qa = z[:, 0:Q_LORA]
    kva = z[:, Q_LORA:Q_LORA + KV_LORA]
    qn = _rms(qa, gq_ref[...]).astype(BF16)
    kvn = _rms(kva, gkv_ref[...]).astype(BF16)
    qf = _dot_nt(wq1t_ref[...], qn)
    kn = _dot(kvn, wkn_ref[...])
    vt = _dot_nt(wvt_ref[...], kvn)
    run_wide(n_wide // 2)

    kr_prod = z[:, _KR0:_KR0 + LANES] * tabk_ref[...]
    kr_sum = kr_prod + pltpu.roll(kr_prod, LANES // 2, axis=1)
    lane = lax.broadcasted_iota(jnp.int32, (1, LANES), 1)
    _store_k(k_ref, kn, jnp.where(lane < QK_ROPE, kr_sum, 0.0))
    vt_ref[0] = vt.astype(BF16)

    tabq = tabq_ref[...]
    tq_c, tq_s = tabq[0:LANES], tabq[LANES:2 * LANES]
    half = QK_ROPE // 4
    for h in range(N_HEADS):
        c0 = h * HEAD_PAD
        q_nope = (qf[c0:c0 + QK_NOPE] * Q_SCALE).astype(BF16)
        qr = qf[c0 + QK_NOPE:c0 + HEAD_PAD]
        qr_swap = jnp.concatenate([qr[half:2 * half], qr[0:half], qr[3 * half:4 * half],
                                   qr[2 * half:3 * half], qr[4 * half:]], axis=0)
        q_rope = (qr * tq_c + qr_swap * tq_s).astype(BF16)
        for jb in range(qt_ref.shape[1]):
            qt_ref[0, jb, c0:c0 + QK_NOPE, :] = q_nope[:, jb * Q_BLK:(jb + 1) * Q_BLK]
            qt_ref[0, jb, c0 + QK_NOPE:c0 + HEAD_PAD, :] = q_rope[:, jb * Q_BLK:(jb + 1) * Q_BLK]
        run_wide(n_wide // 2 + ((h + 1) * (n_wide // 3)) // N_HEADS)
    run_wide(n_wide)


def _inproj_call(x, mod3, tabq, tabk, g_mix, g_q, g_kv, wglu, wgate, wlat, wq1t, wkn, wvt, tm):
    B, S, _ = x.shape
    row = lambda w: pl.BlockSpec((1, tm, w), lambda b, i: (b, i, 0))
    col = lambda w: pl.BlockSpec((1, w, tm), lambda b, i: (b, 0, i))
    return pl.pallas_call(
        _inproj_kernel,
        grid=(B, S // tm),
        in_specs=[row(D_MODEL),
                  pl.BlockSpec((1, 1, 6 * D_MODEL), lambda b, i: (b, 0, 0)),
                  pl.BlockSpec((2 * LANES, tm), lambda b, i: (0, i)),
                  pl.BlockSpec((tm, LANES), lambda b, i: (i, 0)),
                  _resident((1, D_MODEL)), _resident((1, Q_LORA)), _resident((1, KV_LORA)),
                  _resident(wglu.shape), _resident(wgate.shape), _resident(wlat.shape),
                  _resident(wq1t.shape),
                  _resident(wkn.shape), _resident(wvt.shape)],
        out_specs=[row(D_CONV), row(2 * D_MODEL),
                   pl.BlockSpec((1, tm // Q_BLK, N_HEADS * HEAD_PAD, Q_BLK), lambda b, i: (b, i, 0, 0)),
                   row(N_HEADS * HEAD_PAD), col(N_HEADS * V_DIM)],
        out_shape=[jax.ShapeDtypeStruct((B, S, D_CONV), F32),
                   jax.ShapeDtypeStruct((B, S, 2 * D_MODEL), BF16),
                   jax.ShapeDtypeStruct((B, S // Q_BLK, N_HEADS * HEAD_PAD, Q_BLK), BF16),
                   jax.ShapeDtypeStruct((B, S, N_HEADS * HEAD_PAD), BF16),
                   jax.ShapeDtypeStruct((B, N_HEADS * V_DIM, S), BF16)],
        compiler_params=_params(("arbitrary", "arbitrary")),
        name="inproj",
    )(x, mod3, tabq, tabk, g_mix, g_q, g_kv, wglu, wgate, wlat, wq1t, wkn, wvt)


def _ctxproj_kernel(c_ref, mod_ref, gmix_ref, gkv_ref, wc_ref, wkn_ref, wvt_ref, k_ref, vt_ref):
    m = mod_ref[0]
    shift, scale = m[:, 0:D_MODEL], m[:, D_MODEL:2 * D_MODEL]
    hc = (_rms(c_ref[0], gmix_ref[...]) * (1.0 + scale) + shift).astype(BF16)
    z = _dot(hc, wc_ref[...])
    kvn = _rms(z[:, 0:KV_LORA], gkv_ref[...]).astype(BF16)
    _store_k(k_ref, _dot(kvn, wkn_ref[...]), z[:, KV_LORA:KV_LORA + LANES])
    vt_ref[0] = _dot_nt(wvt_ref[...], kvn).astype(BF16)


def _ctxproj_call(ctx, mod3, g_mix, g_kv, wc, wkn, wvt, ctx_row):
    B, C, _ = ctx.shape
    return pl.pallas_call(
        _ctxproj_kernel,
        grid=(B,),
        in_specs=[pl.BlockSpec((1, C, D_MODEL), lambda b: (b, 0, 0)),
                  pl.BlockSpec((1, 1, 6 * D_MODEL), lambda b: (ctx_row, 0, 0)),
                  _resident((1, D_MODEL)), _resident((1, KV_LORA)),
                  _resident(wc.shape), _resident(wkn.shape), _resident(wvt.shape)],
        out_specs=[pl.BlockSpec((1, C, N_HEADS * HEAD_PAD), lambda b: (b, 0, 0)),
                   pl.BlockSpec((1, N_HEADS * V_DIM, C), lambda b: (b, 0, 0))],
        out_shape=[jax.ShapeDtypeStruct((B, C, N_HEADS * HEAD_PAD), BF16),
                   jax.ShapeDtypeStruct((B, N_HEADS * V_DIM, C), BF16)],
        compiler_params=_params(("arbitrary",)),
        name="ctxproj",
    )(ctx, mod3, g_mix, g_kv, wc, wkn, wvt)


def _attn_kernel(qt_ref, kx_ref, kc_ref, vxt_ref, vct_ref, o_ref, s_ref):
    nq = qt_ref.shape[1]
    n_ctx = kc_ref.shape[1] // K_BLK
    nch = n_ctx + kx_ref.shape[1] // K_BLK

    def k_chunk(h, c):
        cols = slice(h * HEAD_PAD, (h + 1) * HEAD_PAD)
        if c < n_ctx:
            return kc_ref[0, c * K_BLK:(c + 1) * K_BLK, cols]
        return kx_ref[0, (c - n_ctx) * K_BLK:(c - n_ctx + 1) * K_BLK, cols]

    def vt_chunk(h, c):
        rows = slice(h * V_DIM, (h + 1) * V_DIM)
        if c < n_ctx:
            return vct_ref[0, rows, c * K_BLK:(c + 1) * K_BLK]
        return vxt_ref[0, rows, (c - n_ctx) * K_BLK:(c - n_ctx + 1) * K_BLK]

    def fold8(x, op):
        return op(x.reshape(K_BLK // 8, 8, Q_BLK), axis=0)

    items = [(h * nq + j, c) for h in range(ATTN_HEADS) for j in range(nq) for c in range(nch)]
    lead = nch + ATTN_LAG
    n_slots = s_ref.shape[0]
    assert lead < n_slots * nch
    mxs = {}

    def scores(t):
        blk, c = items[t]
        h, j = divmod(blk, nq)
        qt = qt_ref[0, j, h * HEAD_PAD:(h + 1) * HEAD_PAD, :]
        s = _dot(k_chunk(h, c), qt)
        s_ref[blk % n_slots, c] = s
        r = fold8(s, jnp.max)
        mxs[blk] = r if blk not in mxs else jnp.maximum(mxs[blk], r)

    for t in range(min(lead, len(items))):
        scores(t)
    for g, (blk, c) in enumerate(items):
        h, j = divmod(blk, nq)
        if c == 0:
            m = jnp.max(mxs.pop(blk), axis=0, keepdims=True)
            l8, ot = None, None
        if g + lead < len(items):
            scores(g + lead)
        p = jnp.exp2(s_ref[blk % n_slots, c] - m)
        ps = fold8(p, jnp.sum)
        l8 = ps if l8 is None else l8 + ps
        d = _dot(vt_chunk(h, c), p.astype(BF16))
        ot = d if ot is None else ot + d
        if c == nch - 1:
            l = jnp.sum(l8, axis=0, keepdims=True)
            o_ref[0, j * Q_BLK:(j + 1) * Q_BLK, h * V_DIM:(h + 1) * V_DIM] = (
                (ot * (1.0 / l)).T.astype(o_ref.dtype))


def _attn_call(qt, kx, kc, vxt, vct):
    B, S, _ = kx.shape
    C = kc.shape[1]
    assert C % K_BLK == 0 and S % K_BLK == 0 and S % Q_BLK == 0
    nh = ATTN_HEADS
    assert N_HEADS % nh == 0
    return pl.pallas_call(
        _attn_kernel,
        grid=(B, N_HEADS // nh),
        in_specs=[pl.BlockSpec((1, S // Q_BLK, nh * HEAD_PAD, Q_BLK), lambda b, h: (b, 0, h, 0)),
                  pl.BlockSpec((1, S, nh * HEAD_PAD), lambda b, h: (b, 0, h)),
                  pl.BlockSpec((1, C, nh * HEAD_PAD), lambda b, h: (b, 0, h)),
                  pl.BlockSpec((1, nh * V_DIM, S), lambda b, h: (b, h, 0)),
                  pl.BlockSpec((1, nh * V_DIM, C), lambda b, h: (b, h, 0))],
        out_specs=pl.BlockSpec((1, S, nh * V_DIM), lambda b, h: (b, 0, h)),
        out_shape=jax.ShapeDtypeStruct((B, S, N_HEADS * V_DIM), BF16),
        scratch_shapes=[pltpu.VMEM((ATTN_SLOTS, (C + S) // K_BLK, K_BLK, Q_BLK), F32)],
        compiler_params=_params(("arbitrary", "arbitrary")),
        name="attn",
    )(qt, kx, kc, vxt, vct)


def _tail_kernel(*refs, n_seq, n_tiles, cn):
    ring_ref, cv_ref = refs[17], refs[20]
    g = pl.program_id(0)

    @pl.when(g == 0)
    def _():
        ring_ref[...] = jnp.zeros(ring_ref.shape, F32)
        cv_ref[...] = jnp.zeros(cv_ref.shape, BF16)

    @pl.when(g < 2)
    def _():
        _tail_step(g, True, *refs, n_seq=n_seq, n_tiles=n_tiles, cn=cn)

    @pl.when(g >= 2)
    def _():
        _tail_step(g, False, *refs, n_seq=n_seq, n_tiles=n_tiles, cn=cn)


def _tail_step(g, ingest_only,
               u_ref, att_ref, gate_ref, x_ref, mod_ref, w3_ref, b3_ref, lg_ref, lb_ref,
               gffn_ref, gfin_ref, wpw_ref, wo_ref, wout_ref, w13_ref, w2_ref,
               o_ref, ring_ref, pk_ref, y_ref, cv_ref, mg_ref, x1_ref, h2_ref, h_ref,
               *, n_seq, n_tiles, cn):
    tm = x_ref.shape[1]
    d_ff = w2_ref.shape[0]
    sub = D_CONV // LANES

    def ingest():
        pos = lax.rem(g, n_seq)
        keep_prev = jnp.where(jnp.logical_or(pos == 0, g >= n_tiles), 0.0, 1.0).astype(F32)
        keep_next = jnp.where(pos == n_seq - 1, 0.0, 1.0).astype(F32)
        slot, nxt = lax.rem(g, CONV_RING), lax.rem(g + 1, CONV_RING)
        prv = lax.rem(g + CONV_RING - 1, CONV_RING)
        for s in range(sub):
            col = u_ref[0, :, s * LANES:(s + 1) * LANES]
            ring_ref[slot, pl.ds(CONV_HALO * sub + s, tm, stride=sub), :] = col
            ring_ref[prv, pl.ds((CONV_HALO + tm) * sub + s, CONV_HALO, stride=sub), :] = (
                col[0:CONV_HALO] * keep_prev)
            ring_ref[nxt, pl.ds(s, CONV_HALO, stride=sub), :] = col[tm - CONV_HALO:tm] * keep_next

    if ingest_only:
        ingest()
        return

    cslot = lax.rem(g + 2, CONV_RING)
    cpar = lax.rem(g, 2)
    bias, lg, lb = b3_ref[...], lg_ref[...], lb_ref[...]
    off = CONV_HALO - CONV_WIDTH // 2

    half = tm // 2
    for jt in range(half + 2 * CONV_HALO):
        lo = ring_ref[cslot, jt * sub:(jt + 1) * sub, :]
        hi = ring_ref[cslot, (jt + half) * sub:(jt + half + 1) * sub, :]
        pk_ref[jt * 2 * sub:(jt + 1) * 2 * sub, :] = jnp.concatenate([lo, hi], axis=0).astype(BF16)

    def conv_chunk(r, gate):
        acc = jnp.zeros((CONV_ROWS, 2 * sub, LANES), BF16)
        for k in range(CONV_WIDTH):
            t0 = (r * CONV_ROWS + off + k) * 2 * sub
            win = pk_ref[t0:t0 + CONV_ROWS * 2 * sub, :]
            wk = w3_ref[k]
            if k == 0 and gate is not None:
                wk = wk + jnp.concatenate([gate, gate], axis=0).astype(BF16)
            acc = acc + win.reshape(CONV_ROWS, 2 * sub, LANES) * wk
        accf = acc.astype(F32)
        for hb in range(2):
            yh = accf[:, hb * sub:(hb + 1) * sub, :] + bias
            y_ref[2 * r + hb] = yh.reshape(CONV_ROWS * sub, LANES)
            y = jnp.concatenate([y_ref[2 * r + hb, pl.ds(s, CONV_ROWS, stride=sub), :]
                                 for s in range(sub)], axis=-1)
            mu = jnp.mean(y, axis=-1, keepdims=True)
            yc = y - mu
            var = jnp.mean(yc * yc, axis=-1, keepdims=True)
            y = yc * lax.rsqrt(var + EPS) * lg + lb
            out = _silu(y)
            row0 = hb * half + r * CONV_ROWS
            cv_ref[cpar, row0:row0 + CONV_ROWS, :] = out.astype(BF16)
        return _zero_of(out[0:8, 0:LANES])

    m = mod_ref[0]
    gate_mix = m[:, 2 * D_MODEL:3 * D_MODEL]
    shift, scale = m[:, 3 * D_MODEL:4 * D_MODEL], m[:, 4 * D_MODEL:5 * D_MODEL]
    gate_ffn = m[:, 5 * D_MODEL:6 * D_MODEL]
    st = {}

    def anchor(ref, top, tok):
        if tok is not None:
            ref[0:16, 0:LANES] = (top + jnp.concatenate([tok, tok], axis=0)).astype(BF16)

    def merge_chunk(j, tok):
        del tok
        y_conv = _dot(cv_ref[1 - cpar], wpw_ref[:, j:j + cn])
        y_mla = _dot(att_ref[0], wo_ref[:, j:j + cn])
        merged = (gate_ref[0, :, j:j + cn].astype(F32) * y_conv
                  + gate_ref[0, :, D_MODEL + j:D_MODEL + j + cn].astype(F32) * y_mla)
        if j == 0:
            st["mg_top"] = merged[0:16, 0:LANES]
        mg_ref[:, j:j + cn] = merged.astype(BF16)
        return _zero_of(y_conv[0:8, 0:LANES])

    def out_chunk(j, tok):
        anchor(mg_ref, st["mg_top"], tok)
        yo = _dot(mg_ref[...], wout_ref[:, j:j + cn])
        x1_ref[:, j:j + cn] = x_ref[0, :, j:j + cn] + gate_mix[:, j:j + cn] * yo
        return _zero_of(yo[0:8, 0:LANES])

    def ffn_chunk(j, tok):
        if j == 0:
            h2 = _rms(x1_ref[...], gffn_ref[...]) * (1.0 + scale) + shift
            st["h2_top"] = h2[0:16, 0:LANES]
            h2_ref[...] = h2.astype(BF16)
        anchor(h2_ref, st["h2_top"], tok)
        w_ab = jnp.concatenate([w13_ref[:, j:j + LANES], w13_ref[:, d_ff + j:d_ff + j + LANES]], axis=1)
        ab = _dot(h2_ref[...], w_ab)
        hv = _silu(ab[:, 0:LANES]) * ab[:, LANES:]
        if j == 0:
            st["h_top"] = hv[0:16, :]
        h_ref[:, j:j + LANES] = hv.astype(BF16)
        return _zero_of(ab[0:8, 0:LANES])

    def down_chunk(j, tok):
        anchor(h_ref, st["h_top"], tok)
        yd = _dot(h_ref[...], w2_ref[:, j:j + cn])
        x1_ref[:, j:j + cn] = x1_ref[:, j:j + cn] + gate_ffn[:, j:j + cn] * yd
        return _zero_of(yd[0:8, 0:LANES])

    cols = range(0, D_MODEL, cn)
    mxu_tasks = ([functools.partial(merge_chunk, j) for j in cols]
                 + [functools.partial(out_chunk, j) for j in cols]
                 + [functools.partial(ffn_chunk, j) for j in range(0, d_ff, LANES)]
                 + [functools.partial(down_chunk, j) for j in cols])
    cost = ([0] + [2 * D_MODEL * cn] * (len(cols) - 1) + [D_MODEL * cn] * len(cols)
            + [2 * D_MODEL * LANES] * (d_ff // LANES) + [d_ff * cn] * len(cols))
    n_conv = half // CONV_ROWS
    done = 0
    mxu_toks, group_toks = [None] * TAIL_LAG, [None] * TAIL_LAG
    for i, task in enumerate(mxu_tasks):
        want = (sum(cost[:i + 1]) * n_conv) // sum(cost)
        group = None
        for r in range(done, want):
            tok = conv_chunk(r, mxu_toks[-TAIL_LAG])
            group = tok if group is None else group + tok
        done = want
        mxu_toks.append(task(group_toks[-TAIL_LAG]))
        group_toks.append(group)
    o_ref[0] = _rms(x1_ref[...], gfin_ref[...])
    ingest()


def _tail_call(u, att, gates, x, mod3, w3, b3, ln_g, ln_b, g_ffn, g_final,
               w_pw, w_o, w_out, w13, w2, tm, cn):
    B, S, _ = x.shape
    n_seq = S // tm
    n_tiles = B * n_seq
    d_ff = w2.shape[0]
    sub = D_CONV // LANES
    assert tm % CONV_ROWS == 0 and d_ff % cn == 0

    def tile(t):
        return (t // n_seq, lax.rem(t, n_seq), 0)

    in_tile = lambda g: tile(jnp.minimum(g, n_tiles - 1))
    out_t = lambda g: jnp.clip(g - 3, 0, n_tiles - 1)
    out_tile = lambda g: tile(out_t(g))
    row = lambda w, f: pl.BlockSpec((1, tm, w), f)
    return pl.pallas_call(
        functools.partial(_tail_kernel, n_seq=n_seq, n_tiles=n_tiles, cn=cn),
        grid=(n_tiles + 3,),
        in_specs=[row(D_CONV, in_tile), row(N_HEADS * V_DIM, out_tile), row(2 * D_MODEL, out_tile),
                  row(D_MODEL, out_tile),
                  pl.BlockSpec((1, 1, 6 * D_MODEL), lambda g: (out_t(g) // n_seq, 0, 0)),
                  _resident(w3.shape), _resident(b3.shape), _resident((1, D_CONV)),
                  _resident((1, D_CONV)), _resident((1, D_MODEL)), _resident((1, D_MODEL)),
                  _resident(w_pw.shape), _resident(w_o.shape), _resident(w_out.shape),
                  _resident(w13.shape), _resident(w2.shape)],
        out_specs=row(D_MODEL, out_tile),
        out_shape=jax.ShapeDtypeStruct((B, S, D_MODEL), F32),
        scratch_shapes=[pltpu.VMEM((CONV_RING, (tm + 2 * CONV_HALO) * sub, LANES), F32),
                        pltpu.VMEM(((tm // 2 + 2 * CONV_HALO) * 2 * sub, LANES), BF16),
                        pltpu.VMEM((tm // CONV_ROWS, CONV_ROWS * sub, LANES), F32),
                        pltpu.VMEM((2, tm, D_CONV), BF16),
                        pltpu.VMEM((tm, D_MODEL), BF16),
                        pltpu.VMEM((tm, D_MODEL), F32),
                        pltpu.VMEM((tm, D_MODEL), BF16),
                        pltpu.VMEM((tm, d_ff), BF16)],
        compiler_params=_params(("arbitrary",)),
        name="tail",
    )(u, att, gates, x, mod3, w3, b3, ln_g, ln_b, g_ffn, g_final, w_pw, w_o, w_out, w13, w2)


def _rope_tables(S):
    f32 = np.float32
    rows = S // GRID_W
    row = np.repeat(np.arange(rows, dtype=f32), GRID_W)
    col = np.tile(np.arange(GRID_W, dtype=f32), rows)
    axis_dim = QK_ROPE // 2
    inv_freq = (f32(ROPE_THETA) ** (-np.arange(0, axis_dim, 2, dtype=f32) / f32(axis_dim))).astype(f32)
    ar, ac = row[:, None] * inv_freq, col[:, None] * inv_freq
    cos = np.concatenate([np.cos(ar), np.cos(ar), np.cos(ac), np.cos(ac)], axis=-1).astype(f32)
    sin = np.concatenate([-np.sin(ar), np.sin(ar), -np.sin(ac), np.sin(ac)], axis=-1).astype(f32)
    z = np.zeros((S, LANES - QK_ROPE), f32)
    tabq = np.ascontiguousarray((f32(Q_SCALE) * np.concatenate([cos, z, sin, z], axis=-1)).T)
    tabk = np.concatenate([cos, sin], axis=-1)
    return jnp.asarray(tabq), jnp.asarray(tabk)


_ROPE_SWAP = np.concatenate([np.arange(16, 32), np.arange(0, 16),
                             np.arange(48, 64), np.arange(32, 48)])


def _prep_weights(w_in, w_uq, w_ukv):
    off_q, off_kv = 2 * D_CONV, 2 * D_CONV + Q_LORA
    off_kr, off_gate = off_kv + KV_LORA, off_kv + KV_LORA + QK_ROPE
    w_kr = w_in[:, off_kr:off_gate]
    z64 = jnp.zeros((D_MODEL, LANES - QK_ROPE), w_in.dtype)
    wglu = w_in[:, 0:off_q].astype(BF16)
    wgate = w_in[:, off_gate:].astype(BF16)
    wlat = jnp.concatenate([w_in[:, off_q:off_kr], w_kr, w_kr[:, _ROPE_SWAP]],
                           axis=1).astype(BF16)
    wc = jnp.concatenate([w_in[:, off_kv:off_gate], z64], axis=1).astype(BF16)

    uq = w_uq.reshape(Q_LORA, N_HEADS, QK_NOPE + QK_ROPE)
    nope, rope = uq[:, :, :QK_NOPE], uq[:, :, QK_NOPE:]
    zq = jnp.zeros((Q_LORA, N_HEADS, LANES - QK_ROPE), w_uq.dtype)
    wq1t = jnp.concatenate([nope, rope, zq], axis=2).reshape(Q_LORA, N_HEADS * HEAD_PAD).T.astype(BF16)

    ukv = w_ukv.reshape(KV_LORA, N_HEADS, QK_NOPE + V_DIM)
    wkn = ukv[:, :, :QK_NOPE].reshape(KV_LORA, N_HEADS * QK_NOPE).astype(BF16)
    wvt = ukv[:, :, QK_NOPE:].reshape(KV_LORA, N_HEADS * V_DIM).T.astype(BF16)
    return wglu, wgate, wlat, wc, wq1t, wkn, wvt


def kernel(x, c, ctx, c_ctx, w_mod, b_mod, g_mix, g_ffn, w_in, g_q, g_kv, w_uq, w_ukv,
           w_o_mla, w_dw, b_dw, ln_g, ln_b, w_pw, w_out, w_13, w_2, g_final):
    B, S, _ = x.shape
    assert w_mod.shape[0] == 1, "single-layer block"
    assert B + 1 <= MOD_ROWS

    wglu, wgate, wlat, wc, wq1t, wkn, wvt = _prep_weights(w_in[0], w_uq[0], w_ukv[0])
    tabq, tabk = _rope_tables(S)
    w3 = w_dw[0].reshape(CONV_WIDTH, D_CONV // LANES, LANES)
    w3 = jnp.concatenate([w3, w3], axis=1).astype(BF16)
    b3 = b_dw.reshape(D_CONV // LANES, LANES)

    cc = jnp.zeros((MOD_ROWS, D_MODEL), F32).at[0:B].set(c).at[B].set(c_ctx)
    mod3 = _mod_call(cc, w_mod[0], b_mod).reshape(MOD_ROWS, 1, 6 * D_MODEL)

    u, gates, qt, kx, vxt = _inproj_call(x, mod3, tabq, tabk, g_mix, g_q, g_kv,
                                         wglu, wgate, wlat, wq1t, wkn, wvt, tm=256)
    kc, vct = _ctxproj_call(ctx, mod3, g_mix, g_kv, wc, wkn, wvt, ctx_row=B)
    att = _attn_call(qt, kx, kc, vxt, vct)
    return _tail_call(u, att, gates, x, mod3, w3, b3, ln_g, ln_b, g_ffn, g_final.reshape(1, D_MODEL),
                      w_pw[0].astype(BF16), w_o_mla[0].astype(BF16), w_out[0].astype(BF16),
                      w_13[0].astype(BF16), w_2[0].astype(BF16), tm=256, cn=256)
```

```python
import functools

import jax
import jax.numpy as jnp
import numpy as np
from jax import lax
from jax.experimental import pallas as pl
from jax.experimental.pallas import tpu as pltpu

F32 = jnp.float32
BF16 = jnp.bfloat16

D_MODEL = 1024
N_HEADS = 8
QK_NOPE = 128
QK_ROPE = 64
V_DIM = 128
Q_LORA = 384
KV_LORA = 256
D_CONV = 1024
CONV_WIDTH = 31
CONV_HALO = 16
GRID_W = 64
ROPE_THETA = 10000.0
EPS = 1e-6
ATTN_SCALE = (QK_NOPE + QK_ROPE) ** -0.5
Q_SCALE = ATTN_SCALE * float(np.log2(np.e))
HEAD_PAD = 256
Q_BLK = 256
K_BLK = 256
ATTN_LAG = 8
ATTN_SLOTS = 2
ATTN_HEADS = 2
CONV_ROWS = 16
CONV_RING = 4
INPROJ_COLS = 512
TAIL_LAG = 6
LANES = 128
MOD_ROWS = 24

VMEM_LIMIT = 56 * 1024 * 1024

_KR0 = Q_LORA + KV_LORA

_NT = (((1,), (1,)), ((), ()))


def _resident(shape):
    nd = len(shape)
    return pl.BlockSpec(shape, lambda *_: (0,) * nd, pipeline_mode=pl.Buffered(1))


def _params(sem):
    return pltpu.CompilerParams(dimension_semantics=sem, vmem_limit_bytes=VMEM_LIMIT)


def _rms(x, g):
    return x * lax.rsqrt(jnp.mean(x * x, axis=-1, keepdims=True) + EPS) * g


def _silu(x):
    return x * jax.nn.sigmoid(x)


def _dot(a, b):
    return jnp.dot(a, b, preferred_element_type=F32)


def _dot_nt(a, b):
    return lax.dot_general(a, b, _NT, preferred_element_type=F32)


def _zero_of(x):
    u = pltpu.bitcast(x, jnp.uint32)
    u = lax.shift_right_logical(lax.shift_right_logical(u, jnp.uint32(16)), jnp.uint32(16))
    return pltpu.bitcast(u, F32)


def _mod_kernel(cc_ref, w_ref, b_ref, o_ref):
    s = _silu(cc_ref[...])
    o_ref[...] = jnp.dot(s, w_ref[...], preferred_element_type=F32,
                         precision=lax.Precision.HIGHEST) + b_ref[...]


def _mod_call(cc, w_mod, b_mod):
    n = w_mod.shape[1]
    tn = 1024
    return pl.pallas_call(
        _mod_kernel,
        grid=(n // tn,),
        in_specs=[pl.BlockSpec((MOD_ROWS, D_MODEL), lambda j: (0, 0)),
                  pl.BlockSpec((D_MODEL, tn), lambda j: (0, j)),
                  pl.BlockSpec((1, tn), lambda j: (0, j))],
        out_specs=pl.BlockSpec((MOD_ROWS, tn), lambda j: (0, j)),
        out_shape=jax.ShapeDtypeStruct((MOD_ROWS, n), F32),
        compiler_params=_params(("arbitrary",)),
        name="mod",
    )(cc, w_mod, b_mod)


def _store_k(k_ref, kn, kr):
    krb = kr.astype(BF16)
    for h in range(N_HEADS):
        k_ref[0, :, h * HEAD_PAD:h * HEAD_PAD + QK_NOPE] = (
            kn[:, h * QK_NOPE:(h + 1) * QK_NOPE].astype(BF16))
        k_ref[0, :, h * HEAD_PAD + QK_NOPE:(h + 1) * HEAD_PAD] = krb


def _inproj_kernel(x_ref, mod_ref, tabq_ref, tabk_ref, gmix_ref, gq_ref, gkv_ref,
                   wglu_ref, wgate_ref, wlat_ref, wq1t_ref, wkn_ref, wvt_ref,
                   u_ref, gate_ref, qt_ref, k_ref, vt_ref):
    m = mod_ref[0]
    shift, scale = m[:, 0:D_MODEL], m[:, D_MODEL:2 * D_MODEL]
    hx = (_rms(x_ref[0], gmix_ref[...]) * (1.0 + scale) + shift).astype(BF16)

    cn = INPROJ_COLS

    def glu(j):
        a = _dot(hx, wglu_ref[:, j:j + cn])
        g = _dot(hx, wglu_ref[:, D_CONV + j:D_CONV + j + cn])
        u_ref[0, :, j:j + cn] = a * jax.nn.sigmoid(g)

    def gates(j):
        gate_ref[0, :, j:j + cn] = jax.nn.sigmoid(_dot(hx, wgate_ref[:, j:j + cn])).astype(BF16)

    wide = ([functools.partial(glu, j) for j in range(0, D_CONV, cn)]
            + [functools.partial(gates, j) for j in range(0, 2 * D_MODEL, cn)])
    n_wide = len(wide)

    def run_wide(upto):
        while len(wide) > n_wide - upto:
            wide.pop(0)()

    z = _dot(hx, wlat_ref[...])
    run_wide(n_wide // 6)
    qa = z[:, 0:Q_LORA]
    kva = z[:, Q_LORA:Q_LORA + KV_LORA]
    qn = _rms(qa, gq_ref[...]).astype(BF16)
    kvn = _rms(kva, gkv_ref[...]).astype(BF16)
    qf = _dot_nt(wq1t_ref[...], qn)
    kn = _dot(kvn, wkn_ref[...])
    vt = _dot_nt(wvt_ref[...], kvn)
    run_wide(n_wide // 2)

    kr_prod = z[:, _KR0:_KR0 + LANES] * tabk_ref[...]
    kr_sum = kr_prod + pltpu.roll(kr_prod, LANES // 2, axis=1)
    lane = lax.broadcasted_iota(jnp.int32, (1, LANES), 1)
    _store_k(k_ref, kn, jnp.where(lane < QK_ROPE, kr_sum, 0.0))
    vt_ref[0] = vt.astype(BF16)

    tabq = tabq_ref[...]
    tq_c, tq_s = tabq[0:LANES], tabq[LANES:2 * LANES]
    half = QK_ROPE // 4
    for h in range(N_HEADS):
        c0 = h * HEAD_PAD
        q_nope = (qf[c0:c0 + QK_NOPE] * Q_SCALE).astype(BF16)
        qr = qf[c0 + QK_NOPE:c0 + HEAD_PAD]
        qr_swap = jnp.concatenate([qr[half:2 * half], qr[0:half], qr[3 * half:4 * half],
                                   qr[2 * half:3 * half], qr[4 * half:]], axis=0)
        q_rope = (qr * tq_c + qr_swap * tq_s).astype(BF16)
        for jb in range(qt_ref.shape[1]):
            qt_ref[0, jb, c0:c0 + QK_NOPE, :] = q_nope[:, jb * Q_BLK:(jb + 1) * Q_BLK]
            qt_ref[0, jb, c0 + QK_NOPE:c0 + HEAD_PAD, :] = q_rope[:, jb * Q_BLK:(jb + 1) * Q_BLK]
        run_wide(n_wide // 2 + ((h + 1) * (n_wide // 3)) // N_HEADS)
    run_wide(n_wide)


def _inproj_call(x, mod3, tabq, tabk, g_mix, g_q, g_kv, wglu, wgate, wlat, wq1t, wkn, wvt, tm):
    B, S, _ = x.shape
    row = lambda w: pl.BlockSpec((1, tm, w), lambda b, i: (b, i, 0))
    col = lambda w: pl.BlockSpec((1, w, tm), lambda b, i: (b, 0, i))
    return pl.pallas_call(
        _inproj_kernel,
        grid=(B, S // tm),
        in_specs=[row(D_MODEL),
                  pl.BlockSpec((1, 1, 6 * D_MODEL), lambda b, i: (b, 0, 0)),
                  pl.BlockSpec((2 * LANES, tm), lambda b, i: (0, i)),
                  pl.BlockSpec((tm, LANES), lambda b, i: (i, 0)),
                  _resident((1, D_MODEL)), _resident((1, Q_LORA)), _resident((1, KV_LORA)),
                  _resident(wglu.shape), _resident(wgate.shape), _resident(wlat.shape),
                  _resident(wq1t.shape),
                  _resident(wkn.shape), _resident(wvt.shape)],
        out_specs=[row(D_CONV), row(2 * D_MODEL),
                   pl.BlockSpec((1, tm // Q_BLK, N_HEADS * HEAD_PAD, Q_BLK), lambda b, i: (b, i, 0, 0)),
                   row(N_HEADS * HEAD_PAD), col(N_HEADS * V_DIM)],
        out_shape=[jax.ShapeDtypeStruct((B, S, D_CONV), F32),
                   jax.ShapeDtypeStruct((B, S, 2 * D_MODEL), BF16),
                   jax.ShapeDtypeStruct((B, S // Q_BLK, N_HEADS * HEAD_PAD, Q_BLK), BF16),
                   jax.ShapeDtypeStruct((B, S, N_HEADS * HEAD_PAD), BF16),
                   jax.ShapeDtypeStruct((B, N_HEADS * V_DIM, S), BF16)],
        compiler_params=_params(("arbitrary", "arbitrary")),
        name="inproj",
    )(x, mod3, tabq, tabk, g_mix, g_q, g_kv, wglu, wgate, wlat, wq1t, wkn, wvt)


def _ctxproj_kernel(c_ref, mod_ref, gmix_ref, gkv_ref, wc_ref, wkn_ref, wvt_ref, k_ref, vt_ref):
    m = mod_ref[0]
    shift, scale = m[:, 0:D_MODEL], m[:, D_MODEL:2 * D_MODEL]
    hc = (_rms(c_ref[0], gmix_ref[...]) * (1.0 + scale) + shift).astype(BF16)
    z = _dot(hc, wc_ref[...])
    kvn = _rms(z[:, 0:KV_LORA], gkv_ref[...]).astype(BF16)
    _store_k(k_ref, _dot(kvn, wkn_ref[...]), z[:, KV_LORA:KV_LORA + LANES])
    vt_ref[0] = _dot_nt(wvt_ref[...], kvn).astype(BF16)


def _ctxproj_call(ctx, mod3, g_mix, g_kv, wc, wkn, wvt, ctx_row):
    B, C, _ = ctx.shape
    return pl.pallas_call(
        _ctxproj_kernel,
        grid=(B,),
        in_specs=[pl.BlockSpec((1, C, D_MODEL), lambda b: (b, 0, 0)),
                  pl.BlockSpec((1, 1, 6 * D_MODEL), lambda b: (ctx_row, 0, 0)),
                  _resident((1, D_MODEL)), _resident((1, KV_LORA)),
                  _resident(wc.shape), _resident(wkn.shape), _resident(wvt.shape)],
        out_specs=[pl.BlockSpec((1, C, N_HEADS * HEAD_PAD), lambda b: (b, 0, 0)),
                   pl.BlockSpec((1, N_HEADS * V_DIM, C), lambda b: (b, 0, 0))],
        out_shape=[jax.ShapeDtypeStruct((B, C, N_HEADS * HEAD_PAD), BF16),
                   jax.ShapeDtypeStruct((B, N_HEADS * V_DIM, C), BF16)],
        compiler_params=_params(("arbitrary",)),
        name="ctxproj",
    )(ctx, mod3, g_mix, g_kv, wc, wkn, wvt)


def _attn_kernel(qt_ref, kx_ref, kc_ref, vxt_ref, vct_ref, o_ref, s_ref):
    nq = qt_ref.shape[1]
    n_ctx = kc_ref.shape[1] // K_BLK
    nch = n_ctx + kx_ref.shape[1] // K_BLK

    def k_chunk(h, c):
        cols = slice(h * HEAD_PAD, (h + 1) * HEAD_PAD)
        if c < n_ctx:
            return kc_ref[0, c * K_BLK:(c + 1) * K_BLK, cols]
        return kx_ref[0, (c - n_ctx) * K_BLK:(c - n_ctx + 1) * K_BLK, cols]

    def vt_chunk(h, c):
        rows = slice(h * V_DIM, (h + 1) * V_DIM)
        if c < n_ctx:
            return vct_ref[0, rows, c * K_BLK:(c + 1) * K_BLK]
        return vxt_ref[0, rows, (c - n_ctx) * K_BLK:(c - n_ctx + 1) * K_BLK]

    def fold8(x, op):
        return op(x.reshape(K_BLK // 8, 8, Q_BLK), axis=0)

    items = [(h * nq + j, c) for h in range(ATTN_HEADS) for j in range(nq) for c in range(nch)]
    lead = nch + ATTN_LAG
    n_slots = s_ref.shape[0]
    assert lead < n_slots * nch
    mxs = {}

    def scores(t):
        blk, c = items[t]
        h, j = divmod(blk, nq)
        qt = qt_ref[0, j, h * HEAD_PAD:(h + 1) * HEAD_PAD, :]
        s = _dot(k_chunk(h, c), qt)
        s_ref[blk % n_slots, c] = s
        r = fold8(s, jnp.max)
        mxs[blk] = r if blk not in mxs else jnp.maximum(mxs[blk], r)

    for t in range(min(lead, len(items))):
        scores(t)
    for g, (blk, c) in enumerate(items):
        h, j = divmod(blk, nq)
        if c == 0:
            m = jnp.max(mxs.pop(blk), axis=0, keepdims=True)
            l8, ot = None, None
        if g + lead < len(items):
            scores(g + lead)
        p = jnp.exp2(s_ref[blk % n_slots, c] - m)
        ps = fold8(p, jnp.sum)
        l8 = ps if l8 is None else l8 + ps
        d = _dot(vt_chunk(h, c), p.astype(BF16))
        ot = d if ot is None else ot + d
        if c == nch - 1:
            l = jnp.sum(l8, axis=0, keepdims=True)
            o_ref[0, j * Q_BLK:(j + 1) * Q_BLK, h * V_DIM:(h + 1) * V_DIM] = (
                (ot * (1.0 / l)).T.astype(o_ref.dtype))


def _attn_call(qt, kx, kc, vxt, vct):
    B, S, _ = kx.shape
    C = kc.shape[1]
    assert C % K_BLK == 0 and S % K_BLK == 0 and S % Q_BLK == 0
    nh = ATTN_HEADS
    assert N_HEADS % nh == 0
    return pl.pallas_call(
        _attn_kernel,
        grid=(B, N_HEADS // nh),
        in_specs=[pl.BlockSpec((1, S // Q_BLK, nh * HEAD_PAD, Q_BLK), lambda b, h: (b, 0, h, 0)),
                  pl.BlockSpec((1, S, nh * HEAD_PAD), lambda b, h: (b, 0, h)),
                  pl.BlockSpec((1, C, nh * HEAD_PAD), lambda b, h: (b, 0, h)),
                  pl.BlockSpec((1, nh * V_DIM, S), lambda b, h: (b, h, 0)),
                  pl.BlockSpec((1, nh * V_DIM, C), lambda b, h: (b, h, 0))],
        out_specs=pl.BlockSpec((1, S, nh * V_DIM), lambda b, h: (b, 0, h)),
        out_shape=jax.ShapeDtypeStruct((B, S, N_HEADS * V_DIM), BF16),
        scratch_shapes=[pltpu.VMEM((ATTN_SLOTS, (C + S) // K_BLK, K_BLK, Q_BLK), F32)],
        compiler_params=_params(("arbitrary", "arbitrary")),
        name="attn",
    )(qt, kx, kc, vxt, vct)


def _tail_kernel(*refs, n_seq, n_tiles, cn):
    ring_ref, cv_ref = refs[17], refs[20]
    g = pl.program_id(0)

    @pl.when(g == 0)
    def _():
        ring_ref[...] = jnp.zeros(ring_ref.shape, F32)
        cv_ref[...] = jnp.zeros(cv_ref.shape, BF16)

    @pl.when(g < 2)
    def _():
        _tail_step(g, True, *refs, n_seq=n_seq, n_tiles=n_tiles, cn=cn)

    @pl.when(g >= 2)
    def _():
        _tail_step(g, False, *refs, n_seq=n_seq, n_tiles=n_tiles, cn=cn)


def _tail_step(g, ingest_only,
               u_ref, att_ref, gate_ref, x_ref, mod_ref, w3_ref, b3_ref, lg_ref, lb_ref,
               gffn_ref, gfin_ref, wpw_ref, wo_ref, wout_ref, w13_ref, w2_ref,
               o_ref, ring_ref, pk_ref, y_ref, cv_ref, mg_ref, x1_ref, h2_ref, h_ref,
               *, n_seq, n_tiles, cn):
    tm = x_ref.shape[1]
    d_ff = w2_ref.shape[0]
    sub = D_CONV // LANES

    def ingest():
        pos = lax.rem(g, n_seq)
        keep_prev = jnp.where(jnp.logical_or(pos == 0, g >= n_tiles), 0.0, 1.0).astype(F32)
        keep_next = jnp.where(pos == n_seq - 1, 0.0, 1.0).astype(F32)
        slot, nxt = lax.rem(g, CONV_RING), lax.rem(g + 1, CONV_RING)
        prv = lax.rem(g + CONV_RING - 1, CONV_RING)
        for s in range(sub):
            col = u_ref[0, :, s * LANES:(s + 1) * LANES]
            ring_ref[slot, pl.ds(CONV_HALO * sub + s, tm, stride=sub), :] = col
            ring_ref[prv, pl.ds((CONV_HALO + tm) * sub + s, CONV_HALO, stride=sub), :] = (
                col[0:CONV_HALO] * keep_prev)
            ring_ref[nxt, pl.ds(s, CONV_HALO, stride=sub), :] = col[tm - CONV_HALO:tm] * keep_next

    if ingest_only:
        ingest()
        return

    cslot = lax.rem(g + 2, CONV_RING)
    cpar = lax.rem(g, 2)
    bias, lg, lb = b3_ref[...], lg_ref[...], lb_ref[...]
    off = CONV_HALO - CONV_WIDTH // 2

    half = tm // 2
    for jt in range(half + 2 * CONV_HALO):
        lo = ring_ref[cslot, jt * sub:(jt + 1) * sub, :]
        hi = ring_ref[cslot, (jt + half) * sub:(jt + half + 1) * sub, :]
        pk_ref[jt * 2 * sub:(jt + 1) * 2 * sub, :] = jnp.concatenate([lo, hi], axis=0).astype(BF16)

    def conv_chunk(r, gate):
        acc = jnp.zeros((CONV_ROWS, 2 * sub, LANES), BF16)
        for k in range(CONV_WIDTH):
            t0 = (r * CONV_ROWS + off + k) * 2 * sub
            win = pk_ref[t0:t0 + CONV_ROWS * 2 * sub, :]
            wk = w3_ref[k]
            if k == 0 and gate is not None:
                wk = wk + jnp.concatenate([gate, gate], axis=0).astype(BF16)
            acc = acc + win.reshape(CONV_ROWS, 2 * sub, LANES) * wk
        accf = acc.astype(F32)
        for hb in range(2):
            yh = accf[:, hb * sub:(hb + 1) * sub, :] + bias
            y_ref[2 * r + hb] = yh.reshape(CONV_ROWS * sub, LANES)
            y = jnp.concatenate([y_ref[2 * r + hb, pl.ds(s, CONV_ROWS, stride=sub), :]
                                 for s in range(sub)], axis=-1)
            mu = jnp.mean(y, axis=-1, keepdims=True)
            yc = y - mu
            var = jnp.mean(yc * yc, axis=-1, keepdims=True)
            y = yc * lax.rsqrt(var + EPS) * lg + lb
            out = _silu(y)
            row0 = hb * half + r * CONV_ROWS
            cv_ref[cpar, row0:row0 + CONV_ROWS, :] = out.astype(BF16)
        return _zero_of(out[0:8, 0:LANES])

    m = mod_ref[0]
    gate_mix = m[:, 2 * D_MODEL:3 * D_MODEL]
    shift, scale = m[:, 3 * D_MODEL:4 * D_MODEL], m[:, 4 * D_MODEL:5 * D_MODEL]
    gate_ffn = m[:, 5 * D_MODEL:6 * D_MODEL]
    st = {}

    def anchor(ref, top, tok):
        if tok is not None:
            ref[0:16, 0:LANES] = (top + jnp.concatenate([tok, tok], axis=0)).astype(BF16)

    def merge_chunk(j, tok):
        del tok
        y_conv = _dot(cv_ref[1 - cpar], wpw_ref[:, j:j + cn])
        y_mla = _dot(att_ref[0], wo_ref[:, j:j + cn])
        merged = (gate_ref[0, :, j:j + cn].astype(F32) * y_conv
                  + gate_ref[0, :, D_MODEL + j:D_MODEL + j + cn].astype(F32) * y_mla)
        if j == 0:
            st["mg_top"] = merged[0:16, 0:LANES]
        mg_ref[:, j:j + cn] = merged.astype(BF16)
        return _zero_of(y_conv[0:8, 0:LANES])

    def out_chunk(j, tok):
        anchor(mg_ref, st["mg_top"], tok)
        yo = _dot(mg_ref[...], wout_ref[:, j:j + cn])
        x1_ref[:, j:j + cn] = x_ref[0, :, j:j + cn] + gate_mix[:, j:j + cn] * yo
        return _zero_of(yo[0:8, 0:LANES])

    def ffn_chunk(j, tok):
        if j == 0:
            h2 = _rms(x1_ref[...], gffn_ref[...]) * (1.0 + scale) + shift
            st["h2_top"] = h2[0:16, 0:LANES]
            h2_ref[...] = h2.astype(BF16)
        anchor(h2_ref, st["h2_top"], tok)
        w_ab = jnp.concatenate([w13_ref[:, j:j + LANES], w13_ref[:, d_ff + j:d_ff + j + LANES]], axis=1)
        ab = _dot(h2_ref[...], w_ab)
        hv = _silu(ab[:, 0:LANES]) * ab[:, LANES:]
        if j == 0:
            st["h_top"] = hv[0:16, :]
        h_ref[:, j:j + LANES] = hv.astype(BF16)
        return _zero_of(ab[0:8, 0:LANES])

    def down_chunk(j, tok):
        anchor(h_ref, st["h_top"], tok)
        yd = _dot(h_ref[...], w2_ref[:, j:j + cn])
        x1_ref[:, j:j + cn] = x1_ref[:, j:j + cn] + gate_ffn[:, j:j + cn] * yd
        return _zero_of(yd[0:8, 0:LANES])

    cols = range(0, D_MODEL, cn)
    mxu_tasks = ([functools.partial(merge_chunk, j) for j in cols]
                 + [functools.partial(out_chunk, j) for j in cols]
                 + [functools.partial(ffn_chunk, j) for j in range(0, d_ff, LANES)]
                 + [functools.partial(down_chunk, j) for j in cols])
    cost = ([0] + [2 * D_MODEL * cn] * (len(cols) - 1) + [D_MODEL * cn] * len(cols)
            + [2 * D_MODEL * LANES] * (d_ff // LANES) + [d_ff * cn] * len(cols))
    n_conv = half // CONV_ROWS
    done = 0
    mxu_toks, group_toks = [None] * TAIL_LAG, [None] * TAIL_LAG
    for i, task in enumerate(mxu_tasks):
        want = (sum(cost[:i + 1]) * n_conv) // sum(cost)
        group = None
        for r in range(done, want):
            tok = conv_chunk(r, mxu_toks[-TAIL_LAG])
            group = tok if group is None else group + tok
        done = want
        mxu_toks.append(task(group_toks[-TAIL_LAG]))
        group_toks.append(group)
    o_ref[0] = _rms(x1_ref[...], gfin_ref[...])
    ingest()


def _tail_call(u, att, gates, x, mod3, w3, b3, ln_g, ln_b, g_ffn, g_final,
               w_pw, w_o, w_out, w13, w2, tm, cn):
    B, S, _ = x.shape
    n_seq = S // tm
    n_tiles = B * n_seq
    d_ff = w2.shape[0]
    sub = D_CONV // LANES
    assert tm % CONV_ROWS == 0 and d_ff % cn == 0

    def tile(t):
        return (t // n_seq, lax.rem(t, n_seq), 0)

    in_tile = lambda g: tile(jnp.minimum(g, n_tiles - 1))
    out_t = lambda g: jnp.clip(g - 3, 0, n_tiles - 1)
    out_tile = lambda g: tile(out_t(g))
    row = lambda w, f: pl.BlockSpec((1, tm, w), f)
    return pl.pallas_call(
        functools.partial(_tail_kernel, n_seq=n_seq, n_tiles=n_tiles, cn=cn),
        grid=(n_tiles + 3,),
        in_specs=[row(D_CONV, in_tile), row(N_HEADS * V_DIM, out_tile), row(2 * D_MODEL, out_tile),
                  row(D_MODEL, out_tile),
                  pl.BlockSpec((1, 1, 6 * D_MODEL), lambda g: (out_t(g) // n_seq, 0, 0)),
                  _resident(w3.shape), _resident(b3.shape), _resident((1, D_CONV)),
                  _resident((1, D_CONV)), _resident((1, D_MODEL)), _resident((1, D_MODEL)),
                  _resident(w_pw.shape), _resident(w_o.shape), _resident(w_out.shape),
                  _resident(w13.shape), _resident(w2.shape)],
        out_specs=row(D_MODEL, out_tile),
        out_shape=jax.ShapeDtypeStruct((B, S, D_MODEL), F32),
        scratch_shapes=[pltpu.VMEM((CONV_RING, (tm + 2 * CONV_HALO) * sub, LANES), F32),
                        pltpu.VMEM(((tm // 2 + 2 * CONV_HALO) * 2 * sub, LANES), BF16),
                        pltpu.VMEM((tm // CONV_ROWS, CONV_ROWS * sub, LANES), F32),
                        pltpu.VMEM((2, tm, D_CONV), BF16),
                        pltpu.VMEM((tm, D_MODEL), BF16),
                        pltpu.VMEM((tm, D_MODEL), F32),
                        pltpu.VMEM((tm, D_MODEL), BF16),
                        pltpu.VMEM((tm, d_ff), BF16)],
        compiler_params=_params(("arbitrary",)),
        name="tail",
    )(u, att, gates, x, mod3, w3, b3, ln_g, ln_b, g_ffn, g_final, w_pw, w_o, w_out, w13, w2)


def _rope_tables(S):
    f32 = np.float32
    rows = S // GRID_W
    row = np.repeat(np.arange(rows, dtype=f32), GRID_W)
    col = np.tile(np.arange(GRID_W, dtype=f32), rows)
    axis_dim = QK_ROPE // 2
    inv_freq = (f32(ROPE_THETA) ** (-np.arange(0, axis_dim, 2, dtype=f32) / f32(axis_dim))).astype(f32)
    ar, ac = row[:, None] * inv_freq, col[:, None] * inv_freq
    cos = np.concatenate([np.cos(ar), np.cos(ar), np.cos(ac), np.cos(ac)], axis=-1).astype(f32)
    sin = np.concatenate([-np.sin(ar), np.sin(ar), -np.sin(ac), np.sin(ac)], axis=-1).astype(f32)
    z = np.zeros((S, LANES - QK_ROPE), f32)
    tabq = np.ascontiguousarray((f32(Q_SCALE) * np.concatenate([cos, z, sin, z], axis=-1)).T)
    tabk = np.concatenate([cos, sin], axis=-1)
    return jnp.asarray(tabq), jnp.asarray(tabk)


_ROPE_SWAP = np.concatenate([np.arange(16, 32), np.arange(0, 16),
                             np.arange(48, 64), np.arange(32, 48)])


def _prep_weights(w_in, w_uq, w_ukv):
    off_q, off_kv = 2 * D_CONV, 2 * D_CONV + Q_LORA
    off_kr, off_gate = off_kv + KV_LORA, off_kv + KV_LORA + QK_ROPE
    w_kr = w_in[:, off_kr:off_gate]
    z64 = jnp.zeros((D_MODEL, LANES - QK_ROPE), w_in.dtype)
    wglu = w_in[:, 0:off_q].astype(BF16)
    wgate = w_in[:, off_gate:].astype(BF16)
    wlat = jnp.concatenate([w_in[:, off_q:off_kr], w_kr, w_kr[:, _ROPE_SWAP]],
                           axis=1).astype(BF16)
    wc = jnp.concatenate([w_in[:, off_kv:off_gate], z64], axis=1).astype(BF16)

    uq = w_uq.reshape(Q_LORA, N_HEADS, QK_NOPE + QK_ROPE)
    nope, rope = uq[:, :, :QK_NOPE], uq[:, :, QK_NOPE:]
    zq = jnp.zeros((Q_LORA, N_HEADS, LANES - QK_ROPE), w_uq.dtype)
    wq1t = jnp.concatenate([nope, rope, zq], axis=2).reshape(Q_LORA, N_HEADS * HEAD_PAD).T.astype(BF16)

    ukv = w_ukv.reshape(KV_LORA, N_HEADS, QK_NOPE + V_DIM)
    wkn = ukv[:, :, :QK_NOPE].reshape(KV_LORA, N_HEADS * QK_NOPE).astype(BF16)
    wvt = ukv[:, :, QK_NOPE:].reshape(KV_LORA, N_HEADS * V_DIM).T.astype(BF16)
    return wglu, wgate, wlat, wc, wq1t, wkn, wvt


def kernel(x, c, ctx, c_ctx, w_mod, b_mod, g_mix, g_ffn, w_in, g_q, g_kv, w_uq, w_ukv,
           w_o_mla, w_dw, b_dw, ln_g, ln_b, w_pw, w_out, w_13, w_2, g_final):
    B, S, _ = x.shape
    assert w_mod.shape[0] == 1, "single-layer block"
    assert B + 1 <= MOD_ROWS

    wglu, wgate, wlat, wc, wq1t, wkn, wvt = _prep_weights(w_in[0], w_uq[0], w_ukv[0])
    tabq, tabk = _rope_tables(S)
    w3 = w_dw[0].reshape(CONV_WIDTH, D_CONV // LANES, LANES)
    w3 = jnp.concatenate([w3, w3], axis=1).astype(BF16)
    b3 = b_dw.reshape(D_CONV // LANES, LANES)

    cc = jnp.zeros((MOD_ROWS, D_MODEL), F32).at[0:B].set(c).at[B].set(c_ctx)
    mod3 = _mod_call(cc, w_mod[0], b_mod).reshape(MOD_ROWS, 1, 6 * D_MODEL)

    u, gates, qt, kx, vxt = _inproj_call(x, mod3, tabq, tabk, g_mix, g_q, g_kv,
                                         wglu, wgate, wlat, wq1t, wkn, wvt, tm=256)
    kc, vct = _ctxproj_call(ctx, mod3, g_mix, g_kv, wc, wkn, wvt, ctx_row=B)
    att = _attn_call(qt, kx, kc, vxt, vct)
    return _tail_call(u, att, gates, x, mod3, w3, b3, ln_g, ln_b, g_ffn, g_final.reshape(1, D_MODEL),
                      w_pw[0].astype(BF16), w_o_mla[0].astype(BF16), w_out[0].astype(BF16),
                      w_13[0].astype(BF16), w_2[0].astype(BF16), tm=256, cn=256)
```
